```python
import math
import jax, jax.numpy as jnp
from jax import lax
import numpy as np


D_MODEL = 2048
BATCH = 1
SEQ = 16384
DEPTH = 2
DEC_BATCH = 8
DEC_SEQ = 4096
PAST_LEN = 128

GRID_W = 64
Q_BLOCK = 128
ROPE_THETA = 10000.0
MLA_HEADS = (D_MODEL // 2) // 128
MLA_NOPE = 128
MLA_ROPE = 64
MLA_V = 128
Q_LORA = D_MODEL // 4
KV_LORA = D_MODEL // 4
MLA_WIDTH = MLA_HEADS * MLA_V
MLA_SCALE = (MLA_NOPE + MLA_ROPE) ** -0.5
GQA_HD = 128
GQA_HEADS = (D_MODEL // 2) // GQA_HD
GQA_KV_HEADS = 2
GQA_GROUP = GQA_HEADS // GQA_KV_HEADS
GQA_WIDTH = GQA_HEADS * GQA_HD
GQA_SCALE = GQA_HD ** -0.5
AXIAL_DIM = GQA_HD // 2
MIX_WIDTH = MLA_WIDTH + GQA_WIDTH
IN_SPLITS = (Q_LORA,
             Q_LORA + KV_LORA,
             Q_LORA + KV_LORA + MLA_ROPE,
             Q_LORA + KV_LORA + MLA_ROPE + GQA_WIDTH,
             Q_LORA + KV_LORA + MLA_ROPE + GQA_WIDTH + GQA_KV_HEADS * GQA_HD)
IN_COLS = Q_LORA + KV_LORA + MLA_ROPE + GQA_WIDTH + 2 * GQA_KV_HEADS * GQA_HD
N_EXPERTS = 64
TOP_K = 8
N_GROUPS = 8
TOPK_GROUPS = 4
EXPERTS_PER_GROUP = N_EXPERTS // N_GROUPS
D_EXPERT = D_MODEL // 4
ROUTED_SCALE = 2.5
MOE_BLOCK = 128
PLE_DIM = 256
ALPHA = (2 * DEPTH) ** 0.25
BETA = (8 * DEPTH) ** -0.25
RMS_EPS = 1e-6
LN_EPS = 1e-5

kernel_name = 'hybrid_mla_axialgqa_moe_deepnorm_encoder'


def rms_norm(x, g):
    xf = x.astype(jnp.float32)
    y = xf * lax.rsqrt(jnp.mean(xf * xf, axis=-1, keepdims=True) + RMS_EPS) * g
    return y.astype(x.dtype)


def layer_norm(x, g, b):
    xf = x.astype(jnp.float32)
    mu = jnp.mean(xf, axis=-1, keepdims=True)
    xc = xf - mu
    var = jnp.mean(xc * xc, axis=-1, keepdims=True)
    return (xc * lax.rsqrt(var + LN_EPS) * g + b).astype(x.dtype)


def rope_angles(pos, dim):
    inv = ROPE_THETA ** (-jnp.arange(0, dim, 2, dtype=jnp.float32) / dim)
    return pos[:, None] * inv[None, :]


def apply_rope(x, ang):
    shape = (1, ang.shape[0]) + (1,) * (x.ndim - 3) + (ang.shape[1],)
    cos = jnp.cos(ang).reshape(shape)
    sin = jnp.sin(ang).reshape(shape)
    x1, x2 = jnp.split(x.astype(jnp.float32), 2, axis=-1)
    return jnp.concatenate([x1 * cos - x2 * sin, x2 * cos + x1 * sin], axis=-1).astype(x.dtype)


def axial_rope(x, ang_row, ang_col):
    return jnp.concatenate([apply_rope(x[..., :AXIAL_DIM], ang_row),
                            apply_rope(x[..., AXIAL_DIM:], ang_col)], axis=-1)


def position_angles(n):
    rows = n // GRID_W
    t = jnp.arange(n, dtype=jnp.float32)
    row = jnp.repeat(jnp.arange(rows, dtype=jnp.float32), GRID_W)
    col = jnp.tile(jnp.arange(GRID_W, dtype=jnp.float32), rows)
    return rope_angles(t, MLA_ROPE), rope_angles(row, AXIAL_DIM), rope_angles(col, AXIAL_DIM)


def query_blocks(a):
    b, s = a.shape[:2]
    return jnp.moveaxis(a.reshape((b, s // Q_BLOCK, Q_BLOCK) + a.shape[2:]), 1, 0)


def merge_blocks(o):
    o = jnp.moveaxis(o, 0, 1)
    return o.reshape((o.shape[0], o.shape[1] * o.shape[2]) + o.shape[3:])


def mla_attention(q_nope, q_rope, k_nope, k_rope, v):
    def one_block(blk):
        qn, qr = blk
        s = (jnp.einsum('bqhd,bkhd->bhqk', qn, k_nope, preferred_element_type=jnp.float32)
             + jnp.einsum('bqhd,bkd->bhqk', qr, k_rope, preferred_element_type=jnp.float32)) * MLA_SCALE
        p = jax.nn.softmax(s, axis=-1).astype(v.dtype)
        return jnp.einsum('bhqk,bkhd->bqhd', p, v)
    return merge_blocks(lax.map(one_block, (query_blocks(q_nope), query_blocks(q_rope))))


def gqa_attention(q, k, v):
    def one_block(qb):
        s = jnp.einsum('bqngd,bknd->bngqk', qb, k, preferred_element_type=jnp.float32) * GQA_SCALE
        p = jax.nn.softmax(s, axis=-1).astype(v.dtype)
        return jnp.einsum('bngqk,bknd->bqngd', p, v)
    return merge_blocks(lax.map(one_block, query_blocks(q)))


def token_mixer(x, ang_seq, ang_row, ang_col, w_in, g_q_lat, g_kv_lat, w_uq, w_uk, w_uv,
                g_q_gqa, g_k_gqa, g_o_mla, g_o_gqa, w_o):
    b, s, _ = x.shape
    c_q, c_kv, k_rope, q_g, k_g, v_g = jnp.split(x @ w_in, IN_SPLITS, axis=-1)
    q = (rms_norm(c_q, g_q_lat) @ w_uq).reshape(b, s, MLA_HEADS, MLA_NOPE + MLA_ROPE)
    q_nope = q[..., :MLA_NOPE]
    q_rope = apply_rope(q[..., MLA_NOPE:], ang_seq)
    kv_lat = rms_norm(c_kv, g_kv_lat)
    k_nope = (kv_lat @ w_uk).reshape(b, s, MLA_HEADS, MLA_NOPE)
    v = (kv_lat @ w_uv).reshape(b, s, MLA_HEADS, MLA_V)
    k_rope = apply_rope(k_rope, ang_seq)
    o_mla = mla_attention(q_nope, q_rope, k_nope, k_rope, v).reshape(b, s, MLA_WIDTH)
    q_g = axial_rope(rms_norm(q_g.reshape(b, s, GQA_KV_HEADS, GQA_GROUP, GQA_HD), g_q_gqa), ang_row, ang_col)
    k_g = axial_rope(rms_norm(k_g.reshape(b, s, GQA_KV_HEADS, GQA_HD), g_k_gqa), ang_row, ang_col)
    v_g = v_g.reshape(b, s, GQA_KV_HEADS, GQA_HD)
    o_gqa = gqa_attention(q_g, k_g, v_g).reshape(b, s, GQA_WIDTH)
    o = jnp.concatenate([rms_norm(o_mla, g_o_mla), rms_norm(o_gqa, g_o_gqa)], axis=-1)
    return o @ w_o


def moe_ffn(h, w_router, router_bias, w_gate, w_up, w_down, ws_gate, ws_up, ws_down):
    b, s, d = h.shape
    n_tok = b * s
    tok = h.reshape(n_tok, d)
    scores = jax.nn.sigmoid(jnp.einsum('td,de->te', tok, w_router, preferred_element_type=jnp.float32))
    sel = scores + router_bias.astype(jnp.float32)
    grp_score = lax.top_k(sel.reshape(n_tok, N_GROUPS, EXPERTS_PER_GROUP), 2)[0].sum(-1)
    _, top_grp = lax.top_k(grp_score, TOPK_GROUPS)
    grp_mask = jnp.any(top_grp[:, :, None] == jnp.arange(N_GROUPS)[None, None, :], axis=1)
    sel = jnp.where(jnp.repeat(grp_mask, EXPERTS_PER_GROUP, axis=1), sel, -jnp.inf)
    _, top_e = lax.top_k(sel, TOP_K)
    gate = jnp.take_along_axis(scores, top_e, axis=1)
    gate = gate / jnp.sum(gate, axis=-1, keepdims=True) * ROUTED_SCALE
    n_assign = n_tok * TOP_K
    flat_e = top_e.reshape(-1)
    order = jnp.argsort(flat_e)
    e_sorted = flat_e[order]
    tok_sorted = (order // TOP_K).astype(jnp.int32)
    w_sorted = gate.reshape(-1)[order]
    counts = jnp.bincount(flat_e, length=N_EXPERTS)
    padded = (counts + MOE_BLOCK - 1) // MOE_BLOCK * MOE_BLOCK
    pad_end = jnp.cumsum(padded)
    pad_start = pad_end - padded
    grp_start = jnp.cumsum(counts) - counts
    dest = pad_start[e_sorted] + jnp.arange(n_assign) - grp_start[e_sorted]
    n_blocks = -(-n_assign // MOE_BLOCK) + N_EXPERTS
    n_slots = n_blocks * MOE_BLOCK
    slot_tok = jnp.zeros((n_slots,), jnp.int32).at[dest].set(tok_sorted)
    slot_w = jnp.zeros((n_slots,), jnp.float32).at[dest].set(w_sorted)
    blk_e = jnp.minimum(jnp.searchsorted(pad_end, jnp.arange(n_blocks) * MOE_BLOCK, side='right'),
                        N_EXPERTS - 1)

    def expert_block(acc, blk):
        idx, wt, e = blk
        xb = tok[idx]
        hb = jax.nn.silu(xb @ w_gate[e]) * (xb @ w_up[e])
        yb = (hb @ w_down[e]).astype(jnp.float32)
        return acc.at[idx].add(yb * wt[:, None]), None

    routed, _ = lax.scan(expert_block, jnp.zeros((n_tok, d), jnp.float32),
                         (slot_tok.reshape(n_blocks, MOE_BLOCK), slot_w.reshape(n_blocks, MOE_BLOCK), blk_e))
    shared = (jax.nn.silu(tok @ ws_gate) * (tok @ ws_up)) @ ws_down
    return (routed + shared.astype(jnp.float32)).astype(h.dtype).reshape(b, s, d)


def setup_inputs(seed: int = 0) -> dict:
    key = jax.random.key(seed)
    ks = iter(jax.random.split(key, 40))
    f32 = jnp.float32
    L, D = DEPTH, D_MODEL

    def nrm(shape, scale):
        return jax.random.normal(next(ks), shape, f32) * scale

    def gain(shape):
        return 1.0 + 0.02 * jax.random.normal(next(ks), shape, f32)

    return {
        'x_prompt': jax.random.normal(next(ks), (BATCH, SEQ, D), f32),
        'x_sample': jax.random.normal(next(ks), (DEC_BATCH, DEC_SEQ, D), f32),
        'p_prompt': jax.random.normal(next(ks), (DEPTH, BATCH, SEQ, PLE_DIM), f32),
        'p_sample': jax.random.normal(next(ks), (DEPTH, DEC_BATCH, DEC_SEQ, PLE_DIM), f32),
        'w_in': nrm((L, D, IN_COLS), D ** -0.5),
        'g_q_lat': gain((L, Q_LORA)),
        'g_kv_lat': gain((L, KV_LORA)),
        'w_uq': nrm((L, Q_LORA, MLA_HEADS * (MLA_NOPE + MLA_ROPE)), Q_LORA ** -0.5),
        'w_uk': nrm((L, KV_LORA, MLA_HEADS * MLA_NOPE), KV_LORA ** -0.5),
        'w_uv': nrm((L, KV_LORA, MLA_HEADS * MLA_V), KV_LORA ** -0.5),
        'g_q_gqa': gain((L, GQA_HD)),
        'g_k_gqa': gain((L, GQA_HD)),
        'g_o_mla': gain((L, MLA_WIDTH)),
        'g_o_gqa': gain((L, GQA_WIDTH)),
        'w_o': nrm((L, MIX_WIDTH, D), MIX_WIDTH ** -0.5 * BETA),
        'ln1_g': gain((L, D)),
        'ln1_b': nrm((L, D), 0.02),
        'w_router': nrm((L, D, N_EXPERTS), D ** -0.5),
        'router_bias': nrm((L, N_EXPERTS), 0.01),
        'w_gate': nrm((L, N_EXPERTS, D, D_EXPERT), D ** -0.5),
        'w_up': nrm((L, N_EXPERTS, D, D_EXPERT), D ** -0.5),
        'w_down': nrm((L, N_EXPERTS, D_EXPERT, D), D_EXPERT ** -0.5 * BETA),
        'ws_gate': nrm((L, D, D_EXPERT), D ** -0.5),
        'ws_up': nrm((L, D, D_EXPERT), D ** -0.5),
        'ws_down': nrm((L, D_EXPERT, D), D_EXPERT ** -0.5 * BETA),
        'w_ple_gate': nrm((L, D, D), D ** -0.5),
        'w_ple': nrm((L, PLE_DIM, D), PLE_DIM ** -0.5 * BETA),
        'ln2_g': gain((L, D)),
        'ln2_b': nrm((L, D), 0.02),
    }


def reference(x_prompt, x_sample, p_prompt, p_sample, w_in, g_q_lat, g_kv_lat, w_uq, w_uk, w_uv,
              g_q_gqa, g_k_gqa, g_o_mla, g_o_gqa, w_o, ln1_g, ln1_b, w_router, router_bias,
              w_gate, w_up, w_down, ws_gate, ws_up, ws_down, w_ple_gate, w_ple, ln2_g, ln2_b):
    def trunk(x, p):
        ang_seq, ang_row, ang_col = position_angles(x.shape[1])
        for i in range(DEPTH):
            mix = token_mixer(x, ang_seq, ang_row, ang_col, w_in[i], g_q_lat[i], g_kv_lat[i],
                              w_uq[i], w_uk[i], w_uv[i], g_q_gqa[i], g_k_gqa[i],
                              g_o_mla[i], g_o_gqa[i], w_o[i])
            x = layer_norm(ALPHA * x + mix, ln1_g[i], ln1_b[i])
            ch = moe_ffn(x, w_router[i], router_bias[i], w_gate[i], w_up[i], w_down[i],
                         ws_gate[i], ws_up[i], ws_down[i])
            ple = jax.nn.sigmoid(x @ w_ple_gate[i]) * (p[i] @ w_ple[i])
            x = layer_norm(ALPHA * x + ch + ple, ln2_g[i], ln2_b[i])
        return x

    y_prompt = trunk(x_prompt, p_prompt)
    y_sample = trunk(x_sample, p_sample)
    return (y_prompt, y_sample)
```

```python
import functools

import jax
import jax.numpy as jnp
from jax import lax
from jax.experimental import pallas as pl
from jax.experimental.pallas import tpu as pltpu

F32 = jnp.float32
BF16 = jnp.bfloat16
U32 = jnp.uint32
I32 = jnp.int32

D_MODEL = 2048
DEPTH = 2
GRID_W = 64
ROPE_THETA = 10000.0
MLA_HEADS = 8
MLA_NOPE = 128
MLA_ROPE = 64
MLA_V = 128
MLA_QK_PAD = 256
Q_LORA = 512
KV_LORA = 512
MLA_WIDTH = MLA_HEADS * MLA_V
MLA_SCALE = (MLA_NOPE + MLA_ROPE) ** -0.5
GQA_HD = 128
GQA_HEADS = 8
GQA_KV_HEADS = 2
GQA_GROUP = GQA_HEADS // GQA_KV_HEADS
GQA_WIDTH = GQA_HEADS * GQA_HD
GQA_SCALE = GQA_HD ** -0.5
N_EXPERTS = 64
TOP_K = 8
N_GROUPS = 8
TOPK_GROUPS = 4
EXPERTS_PER_GROUP = N_EXPERTS // N_GROUPS
D_EXPERT = 512
ROUTED_SCALE = 2.5
PLE_DIM = 256
ALPHA = (2 * DEPTH) ** 0.25
RMS_EPS = 1e-6
LN_EPS = 1e-5
HALF = D_MODEL // 2

LANES = 128
VMEM_LIMIT = 56 * 1024 * 1024

C_CQ = 0
C_CKV = C_CQ + Q_LORA
C_KR = C_CKV + KV_LORA
C_QG = C_KR + LANES
C_KG = C_QG + GQA_WIDTH
C_VG = C_KG + GQA_KV_HEADS * GQA_HD
IN_COLS_PAD = C_VG + GQA_KV_HEADS * GQA_HD


def _tile(n, pref):
    t = min(n, pref)
    while n % t:
        t //= 2
    return t


def _const_spec(shape):
    nd = len(shape)
    return pl.BlockSpec(shape, lambda *_: (0,) * nd, pipeline_mode=pl.Buffered(1))


def _params(sem):
    return pltpu.CompilerParams(dimension_semantics=sem, vmem_limit_bytes=VMEM_LIMIT)


def _pack_pair(lo, hi):
    ulo = lax.bitcast_convert_type(lo.astype(BF16).astype(F32), U32)
    uhi = lax.bitcast_convert_type(hi.astype(BF16).astype(F32), U32)
    return (ulo >> 16) | uhi


def _unpack_pair(u):
    lo = lax.bitcast_convert_type(u << 16, F32)
    hi = lax.bitcast_convert_type(u & jnp.uint32(0xFFFF0000), F32)
    return lo, hi


def _rot_half_partner(x):
    lane = lax.broadcasted_iota(I32, x.shape, x.ndim - 1)
    n = x.shape[-1]
    up = pltpu.roll(x, n - 32, x.ndim - 1)
    down = pltpu.roll(x, 32, x.ndim - 1)
    return jnp.where((lane & 32) == 0, up, down)


def _rope(x, cos, sin_signed):
    return x * cos + _rot_half_partner(x) * sin_signed


def _rms(x, g):
    return x * lax.rsqrt(jnp.mean(x * x, axis=-1, keepdims=True) + RMS_EPS) * g


def _layer_norm(x, g, b):
    mu = jnp.mean(x, axis=-1, keepdims=True)
    xc = x - mu
    var = jnp.mean(xc * xc, axis=-1, keepdims=True)
    return xc * lax.rsqrt(var + LN_EPS) * g + b


def _dot(a, b):
    return jnp.dot(a, b, preferred_element_type=F32)


def _proj_kernel(x_ref, w_in_ref, w_uq_ref, w_uk_ref, w_uv_ref, g_q_lat_ref, g_kv_lat_ref,
                 g_q_gqa_ref, g_k_gqa_ref, cos_m_ref, sin_m_ref, cos_a_ref, sin_a_ref,
                 q_mla_ref, k_mla_ref, v_mla_ref, q_gqa_ref, k_gqa_ref, v_gqa_ref):
    xb = x_ref[...].astype(BF16)
    cos_m, sin_m = cos_m_ref[...], sin_m_ref[...]
    cos_a, sin_a = cos_a_ref[...], sin_a_ref[...]

    cq = _rms(_dot(xb, w_in_ref[:, C_CQ:C_CQ + Q_LORA]), g_q_lat_ref[...]).astype(BF16)
    for h in range(MLA_HEADS):
        c0 = h * MLA_QK_PAD
        q_nope = _dot(cq, w_uq_ref[:, c0:c0 + MLA_NOPE])
        q_rope = _rope(_dot(cq, w_uq_ref[:, c0 + MLA_NOPE:c0 + MLA_QK_PAD]), cos_m, sin_m)
        q_mla_ref[:, c0:c0 + MLA_NOPE] = (q_nope * MLA_SCALE).astype(BF16)
        q_mla_ref[:, c0 + MLA_NOPE:c0 + MLA_QK_PAD] = (q_rope * MLA_SCALE).astype(BF16)

    ckv = _rms(_dot(xb, w_in_ref[:, C_CKV:C_CKV + KV_LORA]), g_kv_lat_ref[...]).astype(BF16)
    k_rope = _rope(_dot(xb, w_in_ref[:, C_KR:C_KR + LANES]), cos_m, sin_m).astype(BF16)
    for h in range(MLA_HEADS):
        c0 = h * MLA_QK_PAD
        k_mla_ref[:, c0:c0 + MLA_NOPE] = _dot(ckv, w_uk_ref[:, h * MLA_NOPE:(h + 1) * MLA_NOPE]).astype(BF16)
        k_mla_ref[:, c0 + MLA_NOPE:c0 + MLA_QK_PAD] = k_rope
    v_mla_ref[...] = _dot(ckv, w_uv_ref[...]).astype(BF16)

    g_q, g_k = g_q_gqa_ref[...], g_k_gqa_ref[...]
    for h in range(GQA_HEADS):
        qh = _dot(xb, w_in_ref[:, C_QG + h * GQA_HD:C_QG + (h + 1) * GQA_HD])
        qh = _rope(_rms(qh, g_q), cos_a, sin_a) * GQA_SCALE
        q_gqa_ref[:, h * GQA_HD:(h + 1) * GQA_HD] = qh.astype(BF16)
    for h in range(GQA_KV_HEADS):
        kh = _dot(xb, w_in_ref[:, C_KG + h * GQA_HD:C_KG + (h + 1) * GQA_HD])
        k_gqa_ref[:, h * GQA_HD:(h + 1) * GQA_HD] = _rope(_rms(kh, g_k), cos_a, sin_a).astype(BF16)
    v_gqa_ref[...] = _dot(xb, w_in_ref[:, C_VG:C_VG + GQA_KV_HEADS * GQA_HD]).astype(BF16)


def _proj(x, wl, tables, pos_block, tm):
    t = x.shape[0]
    row = lambda w: pl.BlockSpec((tm, w), lambda i: (i, 0))
    tab = pl.BlockSpec((tm, LANES), lambda i: (pos_block(i), 0))
    widths = (MLA_HEADS * MLA_QK_PAD, MLA_HEADS * MLA_QK_PAD, MLA_WIDTH,
              GQA_WIDTH, GQA_KV_HEADS * GQA_HD, GQA_KV_HEADS * GQA_HD)
    consts = (wl['w_in_p'], wl['w_uq_p'], wl['w_uk'], wl['w_uv'],
              wl['g_q_lat'], wl['g_kv_lat'], wl['g_q_gqa'], wl['g_k_gqa'])
    return pl.pallas_call(
        _proj_kernel,
        grid=(t // tm,),
        in_specs=[row(D_MODEL)] + [_const_spec(c.shape) for c in consts] + [tab] * 4,
        out_specs=[row(w) for w in widths],
        out_shape=[jax.ShapeDtypeStruct((t, w), BF16) for w in widths],
        compiler_params=_params(("parallel",)),
        name="proj",
    )(x, *consts, *tables)


def _attn_kernel(q_ref, k_ref, v_ref, o_ref, *, tk):
    q = q_ref[...]
    tq = q.shape[0]
    nk = k_ref.shape[0] // tk

    def step(j, carry):
        m, l, acc = carry
        off = pl.multiple_of(j * tk, tk)
        s = lax.dot_general(q, k_ref[pl.ds(off, tk), :], (((1,), (1,)), ((), ())),
                            preferred_element_type=F32)
        m_new = jnp.maximum(m, jnp.max(s, axis=1, keepdims=True))
        alpha = jnp.exp(m - m_new)
        p = jnp.exp(s - m_new)
        l = alpha * l + jnp.sum(p, axis=1, keepdims=True)
        acc = alpha * acc + _dot(p.astype(BF16), v_ref[pl.ds(off, tk), :])
        return m_new, l, acc

    init = (jnp.full((tq, 1), -jnp.inf, F32), jnp.zeros((tq, 1), F32),
            jnp.zeros((tq, v_ref.shape[1]), F32))
    _, l, acc = lax.fori_loop(0, nk, step, init)
    o_ref[...] = (acc / l).astype(o_ref.dtype)


def _attention(q, k, v, *, row0, n_seq, seq, heads, kv_group, dq, dv, tq, tk, name):
    assert row0 % seq == 0 and seq % tq == 0 and seq % tk == 0
    nq = seq // tq
    q0, s0 = row0 // tq, row0 // seq
    return pl.pallas_call(
        functools.partial(_attn_kernel, tk=tk),
        grid=(n_seq, heads, nq),
        in_specs=[pl.BlockSpec((tq, dq), lambda b, h, i: (q0 + b * nq + i, h)),
                  pl.BlockSpec((seq, dq), lambda b, h, i: (s0 + b, h // kv_group)),
                  pl.BlockSpec((seq, dv), lambda b, h, i: (s0 + b, h // kv_group))],
        out_specs=pl.BlockSpec((tq, dv), lambda b, h, i: (b * nq + i, h)),
        out_shape=jax.ShapeDtypeStruct((n_seq * seq, heads * dv), BF16),
        compiler_params=_params(("parallel", "parallel", "parallel")),
        name=name,
    )(q, k, v)


def _outproj_kernel(x_ref, om_ref, og_ref, g_om_ref, g_og_ref, w_o_ref, ln_g_ref, ln_b_ref,
                    x1_ref, x1p_ref):
    om = _rms(om_ref[...].astype(F32), g_om_ref[...]).astype(BF16)
    og = _rms(og_ref[...].astype(F32), g_og_ref[...]).astype(BF16)
    mix = _dot(om, w_o_ref[:MLA_WIDTH, :]) + _dot(og, w_o_ref[MLA_WIDTH:, :])
    x1 = _layer_norm(ALPHA * x_ref[...] + mix, ln_g_ref[...], ln_b_ref[...])
    x1_ref[...] = x1
    x1p_ref[...] = _pack_pair(x1[:, :HALF], x1[:, HALF:])


def _outproj(x, o_mla, o_gqa, wl, tm):
    t = x.shape[0]
    row = lambda w: pl.BlockSpec((tm, w), lambda i: (i, 0))
    consts = (wl['g_o_mla'], wl['g_o_gqa'], wl['w_o'], wl['ln1_g'], wl['ln1_b'])
    return pl.pallas_call(
        _outproj_kernel,
        grid=(t // tm,),
        in_specs=[row(D_MODEL), row(MLA_WIDTH), row(GQA_WIDTH)] + [_const_spec(c.shape) for c in consts],
        out_specs=[row(D_MODEL), row(HALF)],
        out_shape=[jax.ShapeDtypeStruct((t, D_MODEL), F32), jax.ShapeDtypeStruct((t, HALF), U32)],
        compiler_params=_params(("parallel",)),
        name="outproj",
    )(x, o_mla, o_gqa, *consts)


def _beats(a, ia, b, ib):
    return (a > b) | ((a == b) & (ia < ib))


def _route_kernel(x_ref, w_rt_ref, bias_ref, mask_ref, gate_ref, counts_ref):
    logits = lax.dot_general(w_rt_ref[...], x_ref[...], (((1,), (1,)), ((), ())),
                             precision=lax.Precision.HIGHEST, preferred_element_type=F32)
    scores = jax.nn.sigmoid(logits)
    sel = scores + bias_ref[...]
    tm = sel.shape[1]
    neg = jnp.float32(-jnp.inf)

    sub = lax.broadcasted_iota(I32, (EXPERTS_PER_GROUP, tm), 0)
    grp_rows = []
    for g in range(N_GROUPS):
        blk = sel[g * EXPERTS_PER_GROUP:(g + 1) * EXPERTS_PER_GROUP, :]
        m1 = jnp.max(blk, axis=0, keepdims=True)
        first = jnp.min(jnp.where(blk == m1, sub, EXPERTS_PER_GROUP), axis=0, keepdims=True)
        m2 = jnp.max(jnp.where(sub == first, neg, blk), axis=0, keepdims=True)
        grp_rows.append(m1 + m2)
    grp = jnp.concatenate(grp_rows, axis=0)

    gidx = lax.broadcasted_iota(I32, (N_GROUPS, tm), 0)
    grank = jnp.zeros((N_GROUPS, tm), I32)
    for g in range(N_GROUPS):
        grank += _beats(grp[g:g + 1, :], g, grp, gidx).astype(I32)
    gkeep = grank < TOPK_GROUPS
    keep = jnp.concatenate(
        [jnp.broadcast_to(gkeep[g:g + 1, :], (EXPERTS_PER_GROUP, tm)) for g in range(N_GROUPS)], axis=0)
    sel = jnp.where(keep, sel, neg)

    eidx = lax.broadcasted_iota(I32, (N_EXPERTS, tm), 0)
    erank = jnp.zeros((N_EXPERTS, tm), I32)
    for e in range(N_EXPERTS):
        erank += _beats(sel[e:e + 1, :], e, sel, eidx).astype(I32)
    chosen = (erank < TOP_K) & keep

    picked = jnp.where(chosen, scores, 0.0)
    gate = picked / jnp.sum(picked, axis=0, keepdims=True) * ROUTED_SCALE
    mask = chosen.astype(F32)
    mask_ref[...] = mask
    gate_ref[...] = gate

    @pl.when(pl.program_id(0) == 0)
    def _():
        counts_ref[...] = jnp.zeros_like(counts_ref)
    counts_ref[...] += jnp.broadcast_to(jnp.sum(mask, axis=1, keepdims=True), counts_ref.shape)


def _route(x1, w_rt, bias, tm):
    t = x1.shape[0]
    col = pl.BlockSpec((N_EXPERTS, tm), lambda i: (0, i))
    return pl.pallas_call(
        _route_kernel,
        grid=(t // tm,),
        in_specs=[pl.BlockSpec((tm, D_MODEL), lambda i: (i, 0)),
                  _const_spec(w_rt.shape), _const_spec(bias.shape)],
        out_specs=[col, col, pl.BlockSpec((N_EXPERTS, LANES), lambda i: (0, 0))],
        out_shape=[jax.ShapeDtypeStruct((N_EXPERTS, t), F32), jax.ShapeDtypeStruct((N_EXPERTS, t), F32),
                   jax.ShapeDtypeStruct((N_EXPERTS, LANES), F32)],
        compiler_params=_params(("arbitrary",)),
        name="route",
    )(x1, w_rt, bias)


def _slots_kernel(mask_ref, gate_ref, start_ref, slot_ref, w_ref, seen_ref):
    @pl.when(pl.program_id(0) == 0)
    def _():
        seen_ref[...] = jnp.zeros_like(seen_ref)

    mask = mask_ref[...]
    tm = mask.shape[1]
    mb = mask.astype(BF16)
    before = (lax.broadcasted_iota(I32, (tm, tm), 0) < lax.broadcasted_iota(I32, (tm, tm), 1)).astype(BF16)
    pos = _dot(mb, before)
    lower = (lax.broadcasted_iota(I32, (N_EXPERTS, N_EXPERTS), 1)
             < lax.broadcasted_iota(I32, (N_EXPERTS, N_EXPERTS), 0)).astype(BF16)
    rank = _dot(lower, mb)
    seen = seen_ref[:, 0:1]
    slot = start_ref[...] + seen + pos
    on = mask > 0.5
    gate = gate_ref[...]
    for k in range(TOP_K):
        pick = on & (rank == float(k))
        slot_ref[k:k + 1, :] = jnp.sum(jnp.where(pick, slot, 0.0), axis=0, keepdims=True).astype(I32)
        w_ref[k:k + 1, :] = jnp.sum(jnp.where(pick, gate, 0.0), axis=0, keepdims=True)
    seen_ref[...] += jnp.broadcast_to(jnp.sum(mask, axis=1, keepdims=True), seen_ref.shape)


def _slots(mask, gate, start, tm):
    t = mask.shape[1]
    col = pl.BlockSpec((N_EXPERTS, tm), lambda i: (0, i))
    out = pl.BlockSpec((TOP_K, tm), lambda i: (0, i))
    return pl.pallas_call(
        _slots_kernel,
        grid=(t // tm,),
        in_specs=[col, col, _const_spec(start.shape)],
        out_specs=[out, out],
        out_shape=[jax.ShapeDtypeStruct((TOP_K, t), I32), jax.ShapeDtypeStruct((TOP_K, t), F32)],
        scratch_shapes=[pltpu.VMEM((N_EXPERTS, LANES), F32)],
        compiler_params=_params(("arbitrary",)),
        name="slots",
    )(mask, gate, start)


def _dispatch_kernel(fill_start_ref, fill_n_ref, slot_ref, x_hbm, y_hbm, zero_ref, sem, fill_sem, *, tm):
    i = pl.program_id(0)

    def row_copy(t, k):
        return pltpu.make_async_copy(x_hbm.at[pl.ds(i * tm + t, 1)], y_hbm.at[pl.ds(slot_ref[k, t], 1)], sem)

    def issue(t, c):
        for k in range(TOP_K):
            row_copy(t, k).start()
        return c
    lax.fori_loop(0, tm, issue, 0)

    @pl.when(i == 0)
    def _():
        zero_ref[...] = jnp.zeros_like(zero_ref)

        def fill_copy(r):
            return pltpu.make_async_copy(zero_ref, y_hbm.at[pl.ds(r, 1)], fill_sem)

        def per_expert(e, c):
            lax.fori_loop(0, fill_n_ref[e], lambda j, c2: (fill_copy(fill_start_ref[e] + j).start(), c2)[1], 0)
            lax.fori_loop(0, fill_n_ref[e], lambda j, c2: (fill_copy(fill_start_ref[e] + j).wait(), c2)[1], 0)
            return c
        lax.fori_loop(0, N_EXPERTS, per_expert, 0)

    def drain(t, c):
        for k in range(TOP_K):
            row_copy(t, k).wait()
        return c
    lax.fori_loop(0, tm, drain, 0)


def _dispatch(x1p, slot, fill_start, fill_n, n_slots, tm):
    t = x1p.shape[0]
    return pl.pallas_call(
        functools.partial(_dispatch_kernel, tm=tm),
        grid_spec=pltpu.PrefetchScalarGridSpec(
            num_scalar_prefetch=2,
            grid=(t // tm,),
            in_specs=[pl.BlockSpec((TOP_K, tm), lambda i, *_: (0, i), memory_space=pltpu.SMEM),
                      pl.BlockSpec(memory_space=pl.ANY)],
            out_specs=pl.BlockSpec(memory_space=pl.ANY),
            scratch_shapes=[pltpu.VMEM((1, HALF), U32), pltpu.SemaphoreType.DMA, pltpu.SemaphoreType.DMA]),
        out_shape=jax.ShapeDtypeStruct((n_slots, HALF), U32),
        compiler_params=pltpu.CompilerParams(dimension_semantics=("arbitrary",), has_side_effects=True),
        name="dispatch",
    )(fill_start, fill_n, slot, x1p)


def _experts_kernel(blk_e_ref, n_used_ref, y_ref, wg_ref, wu_ref, wd_ref, o_ref):
    @pl.when(pl.program_id(0) < n_used_ref[0])
    def _():
        lo, hi = _unpack_pair(y_ref[...])
        lo, hi = lo.astype(BF16), hi.astype(BF16)
        gate = _dot(lo, wg_ref[0, :HALF, :]) + _dot(hi, wg_ref[0, HALF:, :])
        up = _dot(lo, wu_ref[0, :HALF, :]) + _dot(hi, wu_ref[0, HALF:, :])
        hid = (jax.nn.silu(gate) * up).astype(BF16)
        out = _dot(hid, wd_ref[0])
        o_ref[...] = _pack_pair(out[:, :HALF], out[:, HALF:])


def _experts(y_in, blk_e, n_used, w_gate, w_up, w_down, bm):
    n_slots = y_in.shape[0]
    blk = lambda b, be, nu: (jnp.minimum(b, nu[0] - 1), 0)
    wsel = lambda b, be, nu: (be[b], 0, 0)
    return pl.pallas_call(
        _experts_kernel,
        grid_spec=pltpu.PrefetchScalarGridSpec(
            num_scalar_prefetch=2,
            grid=(n_slots // bm,),
            in_specs=[pl.BlockSpec((bm, HALF), blk),
                      pl.BlockSpec((1, D_MODEL, D_EXPERT), wsel),
                      pl.BlockSpec((1, D_MODEL, D_EXPERT), wsel),
                      pl.BlockSpec((1, D_EXPERT, D_MODEL), wsel)],
            out_specs=pl.BlockSpec((bm, HALF), blk)),
        out_shape=jax.ShapeDtypeStruct((n_slots, HALF), U32),
        compiler_params=_params(("arbitrary",)),
        name="experts",
    )(blk_e, n_used, y_in, w_gate, w_up, w_down)


def _final_kernel(slot_ref, y_hbm, w_ref, x1_ref, p_ref, ws_gate_ref, ws_up_ref, ws_down_ref,
                  w_pg_ref, w_ple_ref, ln_g_ref, ln_b_ref, o_ref, rows_ref, sem, *, tm):
    def row_copy(t, k):
        return pltpu.make_async_copy(y_hbm.at[pl.ds(slot_ref[k, t], 1)],
                                     rows_ref.at[pl.ds(k * tm + t, 1)], sem)

    def issue(t, c):
        for k in range(TOP_K):
            row_copy(t, k).start()
        return c
    lax.fori_loop(0, tm, issue, 0)

    x1 = x1_ref[...]
    xb = x1.astype(BF16)
    hid = (jax.nn.silu(_dot(xb, ws_gate_ref[...])) * _dot(xb, ws_up_ref[...])).astype(BF16)
    shared = _dot(hid, ws_down_ref[...])
    ple = jax.nn.sigmoid(_dot(xb, w_pg_ref[...])) * _dot(p_ref[...].astype(BF16), w_ple_ref[...])
    base = ALPHA * x1 + shared + ple

    def drain(t, c):
        for k in range(TOP_K):
            row_copy(t, k).wait()
        return c
    lax.fori_loop(0, tm, drain, 0)

    w = w_ref[...]
    r_lo = jnp.zeros((tm, HALF), F32)
    r_hi = jnp.zeros((tm, HALF), F32)
    for k in range(TOP_K):
        lo, hi = _unpack_pair(rows_ref[k * tm:(k + 1) * tm, :])
        r_lo += w[:, k:k + 1] * lo
        r_hi += w[:, k:k + 1] * hi
    ln_g, ln_b = ln_g_ref[...], ln_b_ref[...]
    y = jnp.concatenate([base[:, :HALF] + r_lo, base[:, HALF:] + r_hi], axis=1)
    o_ref[...] = _layer_norm(y, ln_g, ln_b)


def _final(slot, y_out, w_tok, x1, p, wl, tm):
    t = x1.shape[0]
    row = lambda w: pl.BlockSpec((tm, w), lambda i: (i, 0))
    consts = (wl['ws_gate'], wl['ws_up'], wl['ws_down'], wl['w_ple_gate'], wl['w_ple'], wl['ln2_g'], wl['ln2_b'])
    return pl.pallas_call(
        functools.partial(_final_kernel, tm=tm),
        grid=(t // tm,),
        in_specs=[pl.BlockSpec((TOP_K, tm), lambda i: (0, i), memory_space=pltpu.SMEM),
                  pl.BlockSpec(memory_space=pl.ANY),
                  row(TOP_K), row(D_MODEL), row(PLE_DIM)] + [_const_spec(c.shape) for c in consts],
        out_specs=row(D_MODEL),
        out_shape=jax.ShapeDtypeStruct((t, D_MODEL), F32),
        scratch_shapes=[pltpu.VMEM((TOP_K * tm, HALF), U32), pltpu.SemaphoreType.DMA],
        compiler_params=_params(("arbitrary",)),
        name="final",
    )(slot, y_out, w_tok, x1, p, *consts)


def _rope_tables(n):
    t = jnp.arange(n, dtype=F32)
    inv = ROPE_THETA ** (-jnp.arange(0, MLA_ROPE, 2, dtype=F32) / MLA_ROPE)
    a_seq = t[:, None] * inv[None, :]
    a_row = jnp.floor(t / GRID_W)[:, None] * inv[None, :]
    a_col = (t - jnp.floor(t / GRID_W) * GRID_W)[:, None] * inv[None, :]
    z = jnp.zeros((n, LANES // 2), F32)
    cos_m = jnp.concatenate([jnp.cos(a_seq), jnp.cos(a_seq), z], axis=1)
    sin_m = jnp.concatenate([-jnp.sin(a_seq), jnp.sin(a_seq), z], axis=1)
    cos_a = jnp.concatenate([jnp.cos(a_row), jnp.cos(a_row), jnp.cos(a_col), jnp.cos(a_col)], axis=1)
    sin_a = jnp.concatenate([-jnp.sin(a_row), jnp.sin(a_row), -jnp.sin(a_col), jnp.sin(a_col)], axis=1)
    return cos_m, sin_m, cos_a, sin_a


def _layer_weights(i, w_in, g_q_lat, g_kv_lat, w_uq, w_uk, w_uv, g_q_gqa, g_k_gqa, g_o_mla, g_o_gqa,
                   w_o, ln1_g, ln1_b, w_router, router_bias, w_gate, w_up, w_down, ws_gate, ws_up,
                   ws_down, w_ple_gate, w_ple, ln2_g, ln2_b):
    wi = w_in[i]
    kr0 = Q_LORA + KV_LORA
    w_in_p = jnp.concatenate(
        [wi[:, :kr0 + MLA_ROPE], jnp.zeros((D_MODEL, LANES - MLA_ROPE), F32), wi[:, kr0 + MLA_ROPE:]], axis=1)
    uq = w_uq[i].reshape(Q_LORA, MLA_HEADS, MLA_NOPE + MLA_ROPE)
    uq = jnp.pad(uq, ((0, 0), (0, 0), (0, MLA_QK_PAD - MLA_NOPE - MLA_ROPE)))
    vec = lambda a: a[i].reshape(1, -1).astype(F32)
    return dict(
        w_in_p=w_in_p.astype(BF16), w_uq_p=uq.reshape(Q_LORA, MLA_HEADS * MLA_QK_PAD).astype(BF16),
        w_uk=w_uk[i].astype(BF16), w_uv=w_uv[i].astype(BF16),
        g_q_lat=vec(g_q_lat), g_kv_lat=vec(g_kv_lat), g_q_gqa=vec(g_q_gqa), g_k_gqa=vec(g_k_gqa),
        g_o_mla=vec(g_o_mla), g_o_gqa=vec(g_o_gqa), w_o=w_o[i].astype(BF16),
        ln1_g=vec(ln1_g), ln1_b=vec(ln1_b),
        w_rt=w_router[i].T.astype(F32), bias=router_bias[i].reshape(-1, 1).astype(F32),
        w_gate=w_gate[i].astype(BF16), w_up=w_up[i].astype(BF16), w_down=w_down[i].astype(BF16),
        ws_gate=ws_gate[i].astype(BF16), ws_up=ws_up[i].astype(BF16), ws_down=ws_down[i].astype(BF16),
        w_ple_gate=w_ple_gate[i].astype(BF16), w_ple=w_ple[i].astype(BF16),
        ln2_g=vec(ln2_g), ln2_b=vec(ln2_b))


def _slot_plan(counts, bm, n_blocks):
    counts = counts.astype(I32)
    padded = (counts + bm - 1) // bm * bm
    pad_end = jnp.cumsum(padded)
    pad_start = pad_end - padded
    blk_e = jnp.minimum(jnp.searchsorted(pad_end, jnp.arange(n_blocks, dtype=I32) * bm, side='right'),
                        N_EXPERTS - 1).astype(I32)
    n_used = jnp.maximum(pad_end[-1:] // bm, 1).astype(I32)
    return pad_start, pad_start + counts, padded - counts, blk_e, n_used


def kernel(x_prompt, x_sample, p_prompt, p_sample, w_in, g_q_lat, g_kv_lat, w_uq, w_uk, w_uv,
           g_q_gqa, g_k_gqa, g_o_mla, g_o_gqa, w_o, ln1_g, ln1_b, w_router, router_bias,
           w_gate, w_up, w_down, ws_gate, ws_up, ws_down, w_ple_gate, w_ple, ln2_g, ln2_b):
    weights = (w_in, g_q_lat, g_kv_lat, w_uq, w_uk, w_uv, g_q_gqa, g_k_gqa, g_o_mla, g_o_gqa, w_o,
               ln1_g, ln1_b, w_router, router_bias, w_gate, w_up, w_down, ws_gate, ws_up, ws_down,
               w_ple_gate, w_ple, ln2_g, ln2_b)
    b1, s1, _ = x_prompt.shape
    b2, s2, _ = x_sample.shape
    t1, t2 = b1 * s1, b2 * s2
    t = t1 + t2
    x = jnp.concatenate([x_prompt.reshape(t1, D_MODEL), x_sample.reshape(t2, D_MODEL)], axis=0)
    p = jnp.concatenate([p_prompt.reshape(DEPTH, t1, PLE_DIM), p_sample.reshape(DEPTH, t2, PLE_DIM)], axis=1)

    tm = _tile(min(s1, s2), 512)
    tm_final = _tile(min(s1, s2), 256)
    tq, tk = _tile(min(s1, s2), 256), _tile(min(s1, s2), 512)
    bm = _tile(t * TOP_K // N_EXPERTS, 512)
    n_blocks = t * TOP_K // bm + N_EXPERTS
    tables = _rope_tables(max(s1, s2))

    def pos_block(i):
        return jnp.where(i < t1 // tm, i % (s1 // tm), (i - t1 // tm) % (s2 // tm))

    for layer in range(DEPTH):
        wl = _layer_weights(layer, *weights)
        q_mla, k_mla, v_mla, q_gqa, k_gqa, v_gqa = _proj(x, wl, tables, pos_block, tm)
        groups = ((0, b1, s1), (t1, b2, s2))
        o_mla = jnp.concatenate([
            _attention(q_mla, k_mla, v_mla, row0=r0, n_seq=nb, seq=s, heads=MLA_HEADS, kv_group=1,
                       dq=MLA_QK_PAD, dv=MLA_V, tq=tq, tk=tk, name="attn_mla")
            for r0, nb, s in groups], axis=0)
        o_gqa = jnp.concatenate([
            _attention(q_gqa, k_gqa, v_gqa, row0=r0, n_seq=nb, seq=s, heads=GQA_HEADS, kv_group=GQA_GROUP,
                       dq=GQA_HD, dv=GQA_HD, tq=tq, tk=tk, name="attn_gqa")
            for r0, nb, s in groups], axis=0)
        x1, x1p = _outproj(x, o_mla, o_gqa, wl, tm)

        mask, gate, counts = _route(x1, wl['w_rt'], wl['bias'], tm)
        pad_start, fill_start, fill_n, blk_e, n_used = _slot_plan(counts[:, 0], bm, n_blocks)
        slot, w_k = _slots(mask, gate, pad_start.astype(F32).reshape(-1, 1), tm)
        y_in = _dispatch(x1p, slot, fill_start, fill_n, n_blocks * bm, tm)
        y_out = _experts(y_in, blk_e, n_used, wl['w_gate'], wl['w_up'], wl['w_down'], bm)
        x = _final(slot, y_out, w_k.T, x1, p[layer], wl, tm_final)

    return x[:t1].reshape(b1, s1, D_MODEL), x[t1:].reshape(b2, s2, D_MODEL)
```

```python
import functools

import jax
import jax.numpy as jnp
from jax import lax
from jax.experimental import pallas as pl
from jax.experimental.pallas import tpu as pltpu

F32 = jnp.float32
BF16 = jnp.bfloat16
U32 = jnp.uint32
I32 = jnp.int32

D_MODEL = 2048
DEPTH = 2
GRID_W = 64
ROPE_THETA = 10000.0
MLA_HEADS = 8
MLA_NOPE = 128
MLA_ROPE = 64
MLA_V = 128
MLA_QK_PAD = 256
Q_LORA = 512
KV_LORA = 512
MLA_WIDTH = MLA_HEADS * MLA_V
MLA_SCALE = (MLA_NOPE + MLA_ROPE) ** -0.5
GQA_HD = 128
GQA_HEADS = 8
GQA_KV_HEADS = 2
GQA_GROUP = GQA_HEADS // GQA_KV_HEADS
GQA_WIDTH = GQA_HEADS * GQA_HD
GQA_SCALE = GQA_HD ** -0.5
N_EXPERTS = 64
TOP_K = 8
N_GROUPS = 8
TOPK_GROUPS = 4
EXPERTS_PER_GROUP = N_EXPERTS // N_GROUPS
D_EXPERT = 512
ROUTED_SCALE = 2.5
PLE_DIM = 256
ALPHA = (2 * DEPTH) ** 0.25
RMS_EPS = 1e-6
LN_EPS = 1e-5
V_EXT = 256
LOG2E = 1.4426950408889634
HALF = D_MODEL // 2

LANES = 128
VMEM_LIMIT = 56 * 1024 * 1024

C_CQ = 0
C_CKV = C_CQ + Q_LORA
C_KR = C_CKV + KV_LORA
C_QG = C_KR + LANES
C_KG = C_QG + GQA_WIDTH
C_VG = C_KG + GQA_KV_HEADS * GQA_HD
IN_COLS_PAD = C_VG + GQA_KV_HEADS * GQA_HD


def _tile(n, pref):
    t = min(n, pref)
    while n % t:
        t //= 2
    return t


def _const_spec(shape):
    nd = len(shape)
    return pl.BlockSpec(shape, lambda *_: (0,) * nd, pipeline_mode=pl.Buffered(1))


def _params(sem):
    return pltpu.CompilerParams(dimension_semantics=sem, vmem_limit_bytes=VMEM_LIMIT)


def _pack_pair(lo, hi):
    ulo = lax.bitcast_convert_type(lo.astype(BF16).astype(F32), U32)
    uhi = lax.bitcast_convert_type(hi.astype(BF16).astype(F32), U32)
    return (ulo >> 16) | uhi


def _unpack_pair(u):
    lo = lax.bitcast_convert_type(u << 16, F32)
    hi = lax.bitcast_convert_type(u & jnp.uint32(0xFFFF0000), F32)
    return lo, hi


def _rot_half_partner(x):
    lane = lax.broadcasted_iota(I32, x.shape, x.ndim - 1)
    n = x.shape[-1]
    up = pltpu.roll(x, n - 32, x.ndim - 1)
    down = pltpu.roll(x, 32, x.ndim - 1)
    return jnp.where((lane & 32) == 0, up, down)


def _rope(x, cos, sin_signed):
    return x * cos + _rot_half_partner(x) * sin_signed


def _rms(x, g):
    return x * lax.rsqrt(jnp.mean(x * x, axis=-1, keepdims=True) + RMS_EPS) * g


def _layer_norm(x, g, b):
    mu = jnp.mean(x, axis=-1, keepdims=True)
    xc = x - mu
    var = jnp.mean(xc * xc, axis=-1, keepdims=True)
    return xc * lax.rsqrt(var + LN_EPS) * g + b


def _dot(a, b):
    return jnp.dot(a, b, preferred_element_type=F32)


def _proj_kernel(x_ref, w_in_ref, w_uq_ref, w_uk_ref, w_uv_ref, g_q_lat_ref, g_kv_lat_ref,
                 g_q_gqa_ref, g_k_gqa_ref, cos_m_ref, sin_m_ref, cos_a_ref, sin_a_ref,
                 q_mla_ref, k_mla_ref, v_mla_ref, q_gqa_ref, k_gqa_ref, v_gqa_ref):
    xb = x_ref[...].astype(BF16)
    cos_m, sin_m = cos_m_ref[...], sin_m_ref[...]
    cos_a, sin_a = cos_a_ref[...], sin_a_ref[...]

    cq = _rms(_dot(xb, w_in_ref[:, C_CQ:C_CQ + Q_LORA]), g_q_lat_ref[...]).astype(BF16)
    for h in range(MLA_HEADS):
        c0 = h * MLA_QK_PAD
        q_nope = _dot(cq, w_uq_ref[:, c0:c0 + MLA_NOPE])
        q_rope = _rope(_dot(cq, w_uq_ref[:, c0 + MLA_NOPE:c0 + MLA_QK_PAD]), cos_m, sin_m)
        q_mla_ref[:, c0:c0 + MLA_NOPE] = (q_nope * (MLA_SCALE * LOG2E)).astype(BF16)
        q_mla_ref[:, c0 + MLA_NOPE:c0 + MLA_QK_PAD] = (q_rope * (MLA_SCALE * LOG2E)).astype(BF16)

    ckv = _rms(_dot(xb, w_in_ref[:, C_CKV:C_CKV + KV_LORA]), g_kv_lat_ref[...]).astype(BF16)
    k_rope = _rope(_dot(xb, w_in_ref[:, C_KR:C_KR + LANES]), cos_m, sin_m).astype(BF16)
    for h in range(MLA_HEADS):
        c0 = h * MLA_QK_PAD
        k_mla_ref[:, c0:c0 + MLA_NOPE] = _dot(ckv, w_uk_ref[:, h * MLA_NOPE:(h + 1) * MLA_NOPE]).astype(BF16)
        k_mla_ref[:, c0 + MLA_NOPE:c0 + MLA_QK_PAD] = k_rope
    ones = jnp.ones((xb.shape[0], V_EXT - MLA_V), BF16)
    for h in range(MLA_HEADS):
        v_mla_ref[:, h * V_EXT:h * V_EXT + MLA_V] = _dot(ckv, w_uv_ref[:, h * MLA_V:(h + 1) * MLA_V]).astype(BF16)
        v_mla_ref[:, h * V_EXT + MLA_V:(h + 1) * V_EXT] = ones

    g_q, g_k = g_q_gqa_ref[...], g_k_gqa_ref[...]
    for h in range(GQA_HEADS):
        qh = _dot(xb, w_in_ref[:, C_QG + h * GQA_HD:C_QG + (h + 1) * GQA_HD])
        qh = _rope(_rms(qh, g_q), cos_a, sin_a) * (GQA_SCALE * LOG2E)
        q_gqa_ref[:, h * GQA_HD:(h + 1) * GQA_HD] = qh.astype(BF16)
    for h in range(GQA_KV_HEADS):
        kh = _dot(xb, w_in_ref[:, C_KG + h * GQA_HD:C_KG + (h + 1) * GQA_HD])
        k_gqa_ref[:, h * GQA_HD:(h + 1) * GQA_HD] = _rope(_rms(kh, g_k), cos_a, sin_a).astype(BF16)
        vh = _dot(xb, w_in_ref[:, C_VG + h * GQA_HD:C_VG + (h + 1) * GQA_HD])
        v_gqa_ref[:, h * V_EXT:h * V_EXT + GQA_HD] = vh.astype(BF16)
        v_gqa_ref[:, h * V_EXT + GQA_HD:(h + 1) * V_EXT] = ones


def _proj(x, wl, tables, pos_block, tm):
    t = x.shape[0]
    row = lambda w: pl.BlockSpec((tm, w), lambda i: (i, 0))
    tab = pl.BlockSpec((tm, LANES), lambda i: (pos_block(i), 0))
    widths = (MLA_HEADS * MLA_QK_PAD, MLA_HEADS * MLA_QK_PAD, MLA_HEADS * V_EXT,
              GQA_WIDTH, GQA_KV_HEADS * GQA_HD, GQA_KV_HEADS * V_EXT)
    consts = (wl['w_in_p'], wl['w_uq_p'], wl['w_uk'], wl['w_uv'],
              wl['g_q_lat'], wl['g_kv_lat'], wl['g_q_gqa'], wl['g_k_gqa'])
    return pl.pallas_call(
        _proj_kernel,
        grid=(t // tm,),
        in_specs=[row(D_MODEL)] + [_const_spec(c.shape) for c in consts] + [tab] * 4,
        out_specs=[row(w) for w in widths],
        out_shape=[jax.ShapeDtypeStruct((t, w), BF16) for w in widths],
        compiler_params=_params(("parallel",)),
        name="proj",
    )(x, *consts, *tables)


def _attn_kernel(q_ref, k_ref, v_ref, o_ref, s_ref, p_ref, a_ref, m_ref, acc_ref, *, tk, dv):
    q = q_ref[...]
    nk = k_ref.shape[0] // tk

    def scores(c, slot):
        off = pl.multiple_of(c * tk, tk)
        s_ref[slot] = lax.dot_general(q, k_ref[pl.ds(off, tk), :], (((1,), (1,)), ((), ())),
                                      preferred_element_type=F32)

    def softmax(slot):
        s = s_ref[slot]
        m = m_ref[...]
        m_new = jnp.maximum(m, jnp.max(s, axis=1, keepdims=True))
        a_ref[slot] = jnp.exp2(m - m_new)
        p_ref[slot] = jnp.exp2(s - m_new).astype(BF16)
        m_ref[...] = m_new

    def values(c, slot):
        off = pl.multiple_of(c * tk, tk)
        acc_ref[...] = a_ref[slot] * acc_ref[...] + _dot(p_ref[slot], v_ref[pl.ds(off, tk), :])

    def tick(t, parity, first=False, last=False):
        if not last:
            scores(t + 1, 1 - parity)
        if not first:
            values(t - 1, 1 - parity)
        softmax(parity)

    m_ref[...] = jnp.full(m_ref.shape, -jnp.inf, F32)
    acc_ref[...] = jnp.zeros(acc_ref.shape, F32)
    scores(0, 0)
    if nk <= 4:
        for t in range(nk):
            tick(t, t % 2, first=t == 0, last=t == nk - 1)
    else:
        assert nk % 2 == 0
        tick(0, 0, first=True)

        def pair(i, c):
            t = 2 * i + 1
            tick(t, 1)
            tick(t + 1, 0)
            return c
        lax.fori_loop(0, (nk - 2) // 2, pair, 0)
        tick(nk - 1, 1, last=True)
    values(nk - 1, (nk - 1) % 2)
    acc = acc_ref[...]
    o_ref[...] = (acc[:, :dv] / acc[:, dv:]).astype(o_ref.dtype)


def _attention(q, k, v, *, row0, n_seq, seq, heads, kv_group, dq, dv, tq, tk, name):
    assert row0 % seq == 0 and seq % tq == 0 and seq % tk == 0 and V_EXT == 2 * dv
    nq = seq // tq
    q0, s0 = row0 // tq, row0 // seq
    return pl.pallas_call(
        functools.partial(_attn_kernel, tk=tk, dv=dv),
        grid=(n_seq, heads, nq),
        in_specs=[pl.BlockSpec((tq, dq), lambda b, h, i: (q0 + b * nq + i, h)),
                  pl.BlockSpec((seq, dq), lambda b, h, i: (s0 + b, h // kv_group)),
                  pl.BlockSpec((seq, V_EXT), lambda b, h, i: (s0 + b, h // kv_group))],
        out_specs=pl.BlockSpec((tq, dv), lambda b, h, i: (b * nq + i, h)),
        out_shape=jax.ShapeDtypeStruct((n_seq * seq, heads * dv), BF16),
        scratch_shapes=[pltpu.VMEM((2, tq, tk), F32), pltpu.VMEM((2, tq, tk), BF16),
                        pltpu.VMEM((2, tq, 1), F32), pltpu.VMEM((tq, 1), F32), pltpu.VMEM((tq, V_EXT), F32)],
        compiler_params=_params(("parallel", "parallel", "parallel")),
        name=name,
    )(q, k, v)


def _outproj_kernel(x_ref, om_ref, og_ref, g_om_ref, g_og_ref, w_o_ref, ln_g_ref, ln_b_ref,
                    x1_ref, x1p_ref):
    om = _rms(om_ref[...].astype(F32), g_om_ref[...]).astype(BF16)
    og = _rms(og_ref[...].astype(F32), g_og_ref[...]).astype(BF16)
    mix = _dot(om, w_o_ref[:MLA_WIDTH, :]) + _dot(og, w_o_ref[MLA_WIDTH:, :])
    x1 = _layer_norm(ALPHA * x_ref[...] + mix, ln_g_ref[...], ln_b_ref[...])
    x1_ref[...] = x1
    x1p_ref[...] = _pack_pair(x1[:, :HALF], x1[:, HALF:])


def _outproj(x, o_mla, o_gqa, wl, tm):
    t = x.shape[0]
    row = lambda w: pl.BlockSpec((tm, w), lambda i: (i, 0))
    consts = (wl['g_o_mla'], wl['g_o_gqa'], wl['w_o'], wl['ln1_g'], wl['ln1_b'])
    return pl.pallas_call(
        _outproj_kernel,
        grid=(t // tm,),
        in_specs=[row(D_MODEL), row(MLA_WIDTH), row(GQA_WIDTH)] + [_const_spec(c.shape) for c in consts],
        out_specs=[row(D_MODEL), row(HALF)],
        out_shape=[jax.ShapeDtypeStruct((t, D_MODEL), F32), jax.ShapeDtypeStruct((t, HALF), U32)],
        compiler_params=_params(("parallel",)),
        name="outproj",
    )(x, o_mla, o_gqa, *consts)


def _beats(a, ia, b, ib):
    return (a > b) | ((a == b) & (ia < ib))


def _route_kernel(x_ref, w_rt_ref, bias_ref, mask_ref, gate_ref, counts_ref):
    logits = lax.dot_general(w_rt_ref[...], x_ref[...], (((1,), (1,)), ((), ())),
                             precision=lax.Precision.HIGHEST, preferred_element_type=F32)
    scores = jax.nn.sigmoid(logits)
    sel = scores + bias_ref[...]
    tm = sel.shape[1]
    neg = jnp.float32(-jnp.inf)

    sub = lax.broadcasted_iota(I32, (EXPERTS_PER_GROUP, tm), 0)
    grp_rows = []
    for g in range(N_GROUPS):
        blk = sel[g * EXPERTS_PER_GROUP:(g + 1) * EXPERTS_PER_GROUP, :]
        m1 = jnp.max(blk, axis=0, keepdims=True)
        first = jnp.min(jnp.where(blk == m1, sub, EXPERTS_PER_GROUP), axis=0, keepdims=True)
        m2 = jnp.max(jnp.where(sub == first, neg, blk), axis=0, keepdims=True)
        grp_rows.append(m1 + m2)
    grp = jnp.concatenate(grp_rows, axis=0)

    gidx = lax.broadcasted_iota(I32, (N_GROUPS, tm), 0)
    grank = jnp.zeros((N_GROUPS, tm), I32)
    for g in range(N_GROUPS):
        grank += _beats(grp[g:g + 1, :], g, grp, gidx).astype(I32)
    gkeep = grank < TOPK_GROUPS
    keep = jnp.concatenate(
        [jnp.broadcast_to(gkeep[g:g + 1, :], (EXPERTS_PER_GROUP, tm)) for g in range(N_GROUPS)], axis=0)
    sel = jnp.where(keep, sel, neg)

    eidx = lax.broadcasted_iota(I32, (N_EXPERTS, tm), 0)
    erank = jnp.zeros((N_EXPERTS, tm), I32)
    for e in range(N_EXPERTS):
        erank += _beats(sel[e:e + 1, :], e, sel, eidx).astype(I32)
    chosen = (erank < TOP_K) & keep

    picked = jnp.where(chosen, scores, 0.0)
    gate = picked / jnp.sum(picked, axis=0, keepdims=True) * ROUTED_SCALE
    mask = chosen.astype(F32)
    mask_ref[...] = mask
    gate_ref[...] = gate

    @pl.when(pl.program_id(0) == 0)
    def _():
        counts_ref[...] = jnp.zeros_like(counts_ref)
    counts_ref[...] += jnp.broadcast_to(jnp.sum(mask, axis=1, keepdims=True), counts_ref.shape)


def _route(x1, w_rt, bias, tm):
    t = x1.shape[0]
    col = pl.BlockSpec((N_EXPERTS, tm), lambda i: (0, i))
    return pl.pallas_call(
        _route_kernel,
        grid=(t // tm,),
        in_specs=[pl.BlockSpec((tm, D_MODEL), lambda i: (i, 0)),
                  _const_spec(w_rt.shape), _const_spec(bias.shape)],
        out_specs=[col, col, pl.BlockSpec((N_EXPERTS, LANES), lambda i: (0, 0))],
        out_shape=[jax.ShapeDtypeStruct((N_EXPERTS, t), F32), jax.ShapeDtypeStruct((N_EXPERTS, t), F32),
                   jax.ShapeDtypeStruct((N_EXPERTS, LANES), F32)],
        compiler_params=_params(("arbitrary",)),
        name="route",
    )(x1, w_rt, bias)


def _slots_kernel(mask_ref, gate_ref, start_ref, slot_ref, w_ref, seen_ref):
    @pl.when(pl.program_id(0) == 0)
    def _():
        seen_ref[...] = jnp.zeros_like(seen_ref)

    mask = mask_ref[...]
    tm = mask.shape[1]
    mb = mask.astype(BF16)
    before = (lax.broadcasted_iota(I32, (tm, tm), 0) < lax.broadcasted_iota(I32, (tm, tm), 1)).astype(BF16)
    pos = _dot(mb, before)
    lower = (lax.broadcasted_iota(I32, (N_EXPERTS, N_EXPERTS), 1)
             < lax.broadcasted_iota(I32, (N_EXPERTS, N_EXPERTS), 0)).astype(BF16)
    rank = _dot(lower, mb)
    seen = seen_ref[:, 0:1]
    slot = start_ref[...] + seen + pos
    on = mask > 0.5
    gate = gate_ref[...]
    for k in range(TOP_K):
        pick = on & (rank == float(k))
        slot_ref[k:k + 1, :] = jnp.sum(jnp.where(pick, slot, 0.0), axis=0, keepdims=True).astype(I32)
        w_ref[k:k + 1, :] = jnp.sum(jnp.where(pick, gate, 0.0), axis=0, keepdims=True)
    seen_ref[...] += jnp.broadcast_to(jnp.sum(mask, axis=1, keepdims=True), seen_ref.shape)


def _slots(mask, gate, start, tm):
    t = mask.shape[1]
    col = pl.BlockSpec((N_EXPERTS, tm), lambda i: (0, i))
    out = pl.BlockSpec((TOP_K, tm), lambda i: (0, i))
    return pl.pallas_call(
        _slots_kernel,
        grid=(t // tm,),
        in_specs=[col, col, _const_spec(start.shape)],
        out_specs=[out, out],
        out_shape=[jax.ShapeDtypeStruct((TOP_K, t), I32), jax.ShapeDtypeStruct((TOP_K, t), F32)],
        scratch_shapes=[pltpu.VMEM((N_EXPERTS, LANES), F32)],
        compiler_params=_params(("arbitrary",)),
        name="slots",
    )(mask, gate, start)


def _dispatch_kernel(fill_start_ref, fill_n_ref, slot_ref, x_ref, y_hbm, zero_ref, sem, fill_sem, *, tm):
    def issue(t, c):
        for k in range(TOP_K):
            pltpu.make_async_copy(x_ref.at[pl.ds(t, 1)], y_hbm.at[pl.ds(slot_ref[k, t], 1)], sem).start()
        return c
    lax.fori_loop(0, tm, issue, 0)

    @pl.when(pl.program_id(0) == 0)
    def _():
        zero_ref[...] = jnp.zeros_like(zero_ref)

        def fill_copy(r):
            return pltpu.make_async_copy(zero_ref, y_hbm.at[pl.ds(r, 1)], fill_sem)

        def per_expert(e, c):
            lax.fori_loop(0, fill_n_ref[e], lambda j, c2: (fill_copy(fill_start_ref[e] + j).start(), c2)[1], 0)
            lax.fori_loop(0, fill_n_ref[e], lambda j, c2: (fill_copy(fill_start_ref[e] + j).wait(), c2)[1], 0)
            return c
        lax.fori_loop(0, N_EXPERTS, per_expert, 0)

    rows = y_hbm.at[pl.ds(0, TOP_K * tm)]
    pltpu.make_async_copy(rows, rows, sem).wait()


def _dispatch(x1p, slot, fill_start, fill_n, n_slots, tm):
    t = x1p.shape[0]
    assert TOP_K * tm <= n_slots
    return pl.pallas_call(
        functools.partial(_dispatch_kernel, tm=tm),
        grid_spec=pltpu.PrefetchScalarGridSpec(
            num_scalar_prefetch=2,
            grid=(t // tm,),
            in_specs=[pl.BlockSpec((TOP_K, tm), lambda i, *_: (0, i), memory_space=pltpu.SMEM),
                      pl.BlockSpec((tm, HALF), lambda i, *_: (i, 0))],
            out_specs=pl.BlockSpec(memory_space=pl.ANY),
            scratch_shapes=[pltpu.VMEM((1, HALF), U32), pltpu.SemaphoreType.DMA, pltpu.SemaphoreType.DMA]),
        out_shape=jax.ShapeDtypeStruct((n_slots, HALF), U32),
        compiler_params=pltpu.CompilerParams(dimension_semantics=("arbitrary",), has_side_effects=True),
        name="dispatch",
    )(fill_start, fill_n, slot, x1p)


def _experts_kernel(blk_e_ref, n_used_ref, y_ref, wg_ref, wu_ref, wd_ref, o_ref):
    @pl.when(pl.program_id(0) < n_used_ref[0])
    def _():
        lo, hi = _unpack_pair(y_ref[...])
        lo, hi = lo.astype(BF16), hi.astype(BF16)
        gate = _dot(lo, wg_ref[0, :HALF, :]) + _dot(hi, wg_ref[0, HALF:, :])
        up = _dot(lo, wu_ref[0, :HALF, :]) + _dot(hi, wu_ref[0, HALF:, :])
        hid = (jax.nn.silu(gate) * up).astype(BF16)
        out = _dot(hid, wd_ref[0])
        o_ref[...] = _pack_pair(out[:, :HALF], out[:, HALF:])


def _experts(y_in, blk_e, n_used, w_gate, w_up, w_down, bm):
    n_slots = y_in.shape[0]
    blk = lambda b, be, nu: (jnp.minimum(b, nu[0] - 1), 0)
    wsel = lambda b, be, nu: (be[b], 0, 0)
    return pl.pallas_call(
        _experts_kernel,
        grid_spec=pltpu.PrefetchScalarGridSpec(
            num_scalar_prefetch=2,
            grid=(n_slots // bm,),
            in_specs=[pl.BlockSpec((bm, HALF), blk),
                      pl.BlockSpec((1, D_MODEL, D_EXPERT), wsel),
                      pl.BlockSpec((1, D_MODEL, D_EXPERT), wsel),
                      pl.BlockSpec((1, D_EXPERT, D_MODEL), wsel)],
            out_specs=pl.BlockSpec((bm, HALF), blk)),
        out_shape=jax.ShapeDtypeStruct((n_slots, HALF), U32),
        compiler_params=_params(("arbitrary",)),
        name="experts",
    )(blk_e, n_used, y_in, w_gate, w_up, w_down)


def _final_kernel(slot_ref, slot_next_ref, y_hbm, w_ref, x1_ref, p_ref, ws_gate_ref, ws_up_ref, ws_down_ref,
                  w_pg_ref, w_ple_ref, ln_g_ref, ln_b_ref, o_ref, rows_ref, sem, *, tm):
    i = pl.program_id(0)
    cur = i % 2

    def gather(ids_ref, buf):
        def issue(t, c):
            for k in range(TOP_K):
                pltpu.make_async_copy(y_hbm.at[pl.ds(ids_ref[k, t], 1)],
                                      rows_ref.at[buf, pl.ds(k * tm + t, 1)], sem.at[buf]).start()
            return c
        lax.fori_loop(0, tm, issue, 0)

    @pl.when(i == 0)
    def _():
        gather(slot_ref, 0)

    @pl.when(i + 1 < pl.num_programs(0))
    def _():
        gather(slot_next_ref, 1 - cur)

    x1 = x1_ref[...]
    xb = x1.astype(BF16)
    hid = (jax.nn.silu(_dot(xb, ws_gate_ref[...])) * _dot(xb, ws_up_ref[...])).astype(BF16)
    shared = _dot(hid, ws_down_ref[...])
    ple = jax.nn.sigmoid(_dot(xb, w_pg_ref[...])) * _dot(p_ref[...].astype(BF16), w_ple_ref[...])
    base = ALPHA * x1 + shared + ple

    pltpu.make_async_copy(y_hbm.at[pl.ds(0, TOP_K * tm)], rows_ref.at[cur], sem.at[cur]).wait()

    w = w_ref[...]
    r_lo = jnp.zeros((tm, HALF), F32)
    r_hi = jnp.zeros((tm, HALF), F32)
    for k in range(TOP_K):
        lo, hi = _unpack_pair(rows_ref[cur, pl.ds(k * tm, tm), :])
        r_lo += w[:, k:k + 1] * lo
        r_hi += w[:, k:k + 1] * hi
    ln_g, ln_b = ln_g_ref[...], ln_b_ref[...]
    y = jnp.concatenate([base[:, :HALF] + r_lo, base[:, HALF:] + r_hi], axis=1)
    o_ref[...] = _layer_norm(y, ln_g, ln_b)


def _final(slot, y_out, w_tok, x1, p, wl, tm):
    t = x1.shape[0]
    n = t // tm
    assert TOP_K * tm <= y_out.shape[0]
    row = lambda w: pl.BlockSpec((tm, w), lambda i: (i, 0))
    consts = (wl['ws_gate'], wl['ws_up'], wl['ws_down'], wl['w_ple_gate'], wl['w_ple'], wl['ln2_g'], wl['ln2_b'])
    return pl.pallas_call(
        functools.partial(_final_kernel, tm=tm),
        grid=(n,),
        in_specs=[pl.BlockSpec((TOP_K, tm), lambda i: (0, i), memory_space=pltpu.SMEM),
                  pl.BlockSpec((TOP_K, tm), lambda i: (0, jnp.minimum(i + 1, n - 1)), memory_space=pltpu.SMEM),
                  pl.BlockSpec(memory_space=pl.ANY),
                  row(TOP_K), row(D_MODEL), row(PLE_DIM)] + [_const_spec(c.shape) for c in consts],
        out_specs=row(D_MODEL),
        out_shape=jax.ShapeDtypeStruct((t, D_MODEL), F32),
        scratch_shapes=[pltpu.VMEM((2, TOP_K * tm, HALF), U32), pltpu.SemaphoreType.DMA((2,))],
        compiler_params=_params(("arbitrary",)),
        name="final",
    )(slot, slot, y_out, w_tok, x1, p, *consts)


def _rope_tables(n):
    t = jnp.arange(n, dtype=F32)
    inv = ROPE_THETA ** (-jnp.arange(0, MLA_ROPE, 2, dtype=F32) / MLA_ROPE)
    a_seq = t[:, None] * inv[None, :]
    a_row = jnp.floor(t / GRID_W)[:, None] * inv[None, :]
    a_col = (t - jnp.floor(t / GRID_W) * GRID_W)[:, None] * inv[None, :]
    z = jnp.zeros((n, LANES // 2), F32)
    cos_m = jnp.concatenate([jnp.cos(a_seq), jnp.cos(a_seq), z], axis=1)
    sin_m = jnp.concatenate([-jnp.sin(a_seq), jnp.sin(a_seq), z], axis=1)
    cos_a = jnp.concatenate([jnp.cos(a_row), jnp.cos(a_row), jnp.cos(a_col), jnp.cos(a_col)], axis=1)
    sin_a = jnp.concatenate([-jnp.sin(a_row), jnp.sin(a_row), -jnp.sin(a_col), jnp.sin(a_col)], axis=1)
    return cos_m, sin_m, cos_a, sin_a


def _layer_weights(i, w_in, g_q_lat, g_kv_lat, w_uq, w_uk, w_uv, g_q_gqa, g_k_gqa, g_o_mla, g_o_gqa,
                   w_o, ln1_g, ln1_b, w_router, router_bias, w_gate, w_up, w_down, ws_gate, ws_up,
                   ws_down, w_ple_gate, w_ple, ln2_g, ln2_b):
    wi = w_in[i]
    kr0 = Q_LORA + KV_LORA
    w_in_p = jnp.concatenate(
        [wi[:, :kr0 + MLA_ROPE], jnp.zeros((D_MODEL, LANES - MLA_ROPE), F32), wi[:, kr0 + MLA_ROPE:]], axis=1)
    uq = w_uq[i].reshape(Q_LORA, MLA_HEADS, MLA_NOPE + MLA_ROPE)
    uq = jnp.pad(uq, ((0, 0), (0, 0), (0, MLA_QK_PAD - MLA_NOPE - MLA_ROPE)))
    vec = lambda a: a[i].reshape(1, -1).astype(F32)
    return dict(
        w_in_p=w_in_p.astype(BF16), w_uq_p=uq.reshape(Q_LORA, MLA_HEADS * MLA_QK_PAD).astype(BF16),
        w_uk=w_uk[i].astype(BF16), w_uv=w_uv[i].astype(BF16),
        g_q_lat=vec(g_q_lat), g_kv_lat=vec(g_kv_lat), g_q_gqa=vec(g_q_gqa), g_k_gqa=vec(g_k_gqa),
        g_o_mla=vec(g_o_mla), g_o_gqa=vec(g_o_gqa), w_o=w_o[i].astype(BF16),
        ln1_g=vec(ln1_g), ln1_b=vec(ln1_b),
        w_rt=w_router[i].T.astype(F32), bias=router_bias[i].reshape(-1, 1).astype(F32),
        w_gate=w_gate[i].astype(BF16), w_up=w_up[i].astype(BF16), w_down=w_down[i].astype(BF16),
        ws_gate=ws_gate[i].astype(BF16), ws_up=ws_up[i].astype(BF16), ws_down=ws_down[i].astype(BF16),
        w_ple_gate=w_ple_gate[i].astype(BF16), w_ple=w_ple[i].astype(BF16),
        ln2_g=vec(ln2_g), ln2_b=vec(ln2_b))


def _slot_plan(counts, bm, n_blocks):
    counts = counts.astype(I32)
    padded = (counts + bm - 1) // bm * bm
    pad_end = jnp.cumsum(padded)
    pad_start = pad_end - padded
    blk_e = jnp.minimum(jnp.searchsorted(pad_end, jnp.arange(n_blocks, dtype=I32) * bm, side='right'),
                        N_EXPERTS - 1).astype(I32)
    n_used = jnp.maximum(pad_end[-1:] // bm, 1).astype(I32)
    return pad_start, pad_start + counts, padded - counts, blk_e, n_used


def kernel(x_prompt, x_sample, p_prompt, p_sample, w_in, g_q_lat, g_kv_lat, w_uq, w_uk, w_uv,
           g_q_gqa, g_k_gqa, g_o_mla, g_o_gqa, w_o, ln1_g, ln1_b, w_router, router_bias,
           w_gate, w_up, w_down, ws_gate, ws_up, ws_down, w_ple_gate, w_ple, ln2_g, ln2_b):
    weights = (w_in, g_q_lat, g_kv_lat, w_uq, w_uk, w_uv, g_q_gqa, g_k_gqa, g_o_mla, g_o_gqa, w_o,
               ln1_g, ln1_b, w_router, router_bias, w_gate, w_up, w_down, ws_gate, ws_up, ws_down,
               w_ple_gate, w_ple, ln2_g, ln2_b)
    b1, s1, _ = x_prompt.shape
    b2, s2, _ = x_sample.shape
    t1, t2 = b1 * s1, b2 * s2
    t = t1 + t2
    x = jnp.concatenate([x_prompt.reshape(t1, D_MODEL), x_sample.reshape(t2, D_MODEL)], axis=0)
    p = jnp.concatenate([p_prompt.reshape(DEPTH, t1, PLE_DIM), p_sample.reshape(DEPTH, t2, PLE_DIM)], axis=1)

    tm = _tile(min(s1, s2), 512)
    tm_final = _tile(min(s1, s2), 256)
    tq, tk = _tile(min(s1, s2), 512), _tile(min(s1, s2), 512)
    bm = _tile(t * TOP_K // N_EXPERTS, 512)
    n_blocks = t * TOP_K // bm + N_EXPERTS
    tables = _rope_tables(max(s1, s2))

    def pos_block(i):
        return jnp.where(i < t1 // tm, i % (s1 // tm), (i - t1 // tm) % (s2 // tm))

    for layer in range(DEPTH):
        wl = _layer_weights(layer, *weights)
        q_mla, k_mla, v_mla, q_gqa, k_gqa, v_gqa = _proj(x, wl, tables, pos_block, tm)
        groups = ((0, b1, s1), (t1, b2, s2))
        o_mla = jnp.concatenate([
            _attention(q_mla, k_mla, v_mla, row0=r0, n_seq=nb, seq=s, heads=MLA_HEADS, kv_group=1,
                       dq=MLA_QK_PAD, dv=MLA_V, tq=tq, tk=tk, name="attn_mla")
            for r0, nb, s in groups], axis=0)
        o_gqa = jnp.concatenate([
            _attention(q_gqa, k_gqa, v_gqa, row0=r0, n_seq=nb, seq=s, heads=GQA_HEADS, kv_group=GQA_GROUP,
                       dq=GQA_HD, dv=GQA_HD, tq=tq, tk=tk, name="attn_gqa")
            for r0, nb, s in groups], axis=0)
        x1, x1p = _outproj(x, o_mla, o_gqa, wl, tm)

        mask, gate, counts = _route(x1, wl['w_rt'], wl['bias'], tm)
        pad_start, fill_start, fill_n, blk_e, n_used = _slot_plan(counts[:, 0], bm, n_blocks)
        slot, w_k = _slots(mask, gate, pad_start.astype(F32).reshape(-1, 1), tm)
        y_in = _dispatch(x1p, slot, fill_start, fill_n, n_blocks * bm, tm)
        y_out = _experts(y_in, blk_e, n_used, wl['w_gate'], wl['w_up'], wl['w_down'], bm)
        x = _final(slot, y_out, w_k.T, x1, p[layer], wl, tm_final)

    return x[:t1].reshape(b1, s1, D_MODEL), x[t1:].reshape(b2, s2, D_MODEL)
```

```python
import functools

import jax
import jax.numpy as jnp
from jax import lax
from jax.experimental import pallas as pl
from jax.experimental.pallas import tpu as pltpu

F32 = jnp.float32
BF16 = jnp.bfloat16
U32 = jnp.uint32
I32 = jnp.int32

D_MODEL = 2048
DEPTH = 2
GRID_W = 64
ROPE_THETA = 10000.0
MLA_HEADS = 8
MLA_NOPE = 128
MLA_ROPE = 64
MLA_V = 128
MLA_QK_PAD = 256
Q_LORA = 512
KV_LORA = 512
MLA_WIDTH = MLA_HEADS * MLA_V
MLA_SCALE = (MLA_NOPE + MLA_ROPE) ** -0.5
GQA_HD = 128
GQA_HEADS = 8
GQA_KV_HEADS = 2
GQA_GROUP = GQA_HEADS // GQA_KV_HEADS
GQA_WIDTH = GQA_HEADS * GQA_HD
GQA_SCALE = GQA_HD ** -0.5
N_EXPERTS = 64
TOP_K = 8
N_GROUPS = 8
TOPK_GROUPS = 4
EXPERTS_PER_GROUP = N_EXPERTS // N_GROUPS
D_EXPERT = 512
ROUTED_SCALE = 2.5
PLE_DIM = 256
ALPHA = (2 * DEPTH) ** 0.25
RMS_EPS = 1e-6
LN_EPS = 1e-5
V_EXT = 256
LOG2E = 1.4426950408889634
TICKS_PER_TRIP = 2
HALF = D_MODEL // 2

LANES = 128
VMEM_LIMIT = 56 * 1024 * 1024

C_CQ = 0
C_CKV = C_CQ + Q_LORA
C_KR = C_CKV + KV_LORA
C_QG = C_KR + LANES
C_KG = C_QG + GQA_WIDTH
C_VG = C_KG + GQA_KV_HEADS * GQA_HD
IN_COLS_PAD = C_VG + GQA_KV_HEADS * GQA_HD


def _tile(n, pref):
    t = min(n, pref)
    while n % t:
        t //= 2
    return t


def _const_spec(shape):
    nd = len(shape)
    return pl.BlockSpec(shape, lambda *_: (0,) * nd, pipeline_mode=pl.Buffered(1))


def _params(sem):
    return pltpu.CompilerParams(dimension_semantics=sem, vmem_limit_bytes=VMEM_LIMIT)


def _pack_pair(lo, hi):
    ulo = lax.bitcast_convert_type(lo.astype(BF16).astype(F32), U32)
    uhi = lax.bitcast_convert_type(hi.astype(BF16).astype(F32), U32)
    return (ulo >> 16) | uhi


def _unpack_pair(u):
    lo = lax.bitcast_convert_type(u << 16, F32)
    hi = lax.bitcast_convert_type(u & jnp.uint32(0xFFFF0000), F32)
    return lo, hi


def _rot_half_partner(x):
    lane = lax.broadcasted_iota(I32, x.shape, x.ndim - 1)
    n = x.shape[-1]
    up = pltpu.roll(x, n - 32, x.ndim - 1)
    down = pltpu.roll(x, 32, x.ndim - 1)
    return jnp.where((lane & 32) == 0, up, down)


def _rope(x, cos, sin_signed):
    return x * cos + _rot_half_partner(x) * sin_signed


def _rms(x, g):
    return x * lax.rsqrt(jnp.mean(x * x, axis=-1, keepdims=True) + RMS_EPS) * g


def _layer_norm(x, g, b):
    mu = jnp.mean(x, axis=-1, keepdims=True)
    xc = x - mu
    var = jnp.mean(xc * xc, axis=-1, keepdims=True)
    return xc * lax.rsqrt(var + LN_EPS) * g + b


def _dot(a, b):
    return jnp.dot(a, b, preferred_element_type=F32)


def _proj_kernel(x_ref, w_in_ref, w_uq_ref, w_uk_ref, w_uv_ref, g_q_lat_ref, g_kv_lat_ref,
                 g_q_gqa_ref, g_k_gqa_ref, cos_m_ref, sin_m_ref, cos_a_ref, sin_a_ref,
                 q_mla_ref, k_mla_ref, v_mla_ref, q_gqa_ref, k_gqa_ref, v_gqa_ref):
    xb = x_ref[...].astype(BF16)
    cos_m, sin_m = cos_m_ref[...], sin_m_ref[...]
    cos_a, sin_a = cos_a_ref[...], sin_a_ref[...]

    cq = _rms(_dot(xb, w_in_ref[:, C_CQ:C_CQ + Q_LORA]), g_q_lat_ref[...]).astype(BF16)
    for h in range(MLA_HEADS):
        c0 = h * MLA_QK_PAD
        q_nope = _dot(cq, w_uq_ref[:, c0:c0 + MLA_NOPE])
        q_rope = _rope(_dot(cq, w_uq_ref[:, c0 + MLA_NOPE:c0 + MLA_QK_PAD]), cos_m, sin_m)
        q_mla_ref[:, c0:c0 + MLA_NOPE] = (q_nope * (MLA_SCALE * LOG2E)).astype(BF16)
        q_mla_ref[:, c0 + MLA_NOPE:c0 + MLA_QK_PAD] = (q_rope * (MLA_SCALE * LOG2E)).astype(BF16)

    ckv = _rms(_dot(xb, w_in_ref[:, C_CKV:C_CKV + KV_LORA]), g_kv_lat_ref[...]).astype(BF16)
    k_rope = _rope(_dot(xb, w_in_ref[:, C_KR:C_KR + LANES]), cos_m, sin_m).astype(BF16)
    for h in range(MLA_HEADS):
        c0 = h * MLA_QK_PAD
        k_mla_ref[:, c0:c0 + MLA_NOPE] = _dot(ckv, w_uk_ref[:, h * MLA_NOPE:(h + 1) * MLA_NOPE]).astype(BF16)
        k_mla_ref[:, c0 + MLA_NOPE:c0 + MLA_QK_PAD] = k_rope
    ones = jnp.ones((xb.shape[0], V_EXT - MLA_V), BF16)
    for h in range(MLA_HEADS):
        v_mla_ref[:, h * V_EXT:h * V_EXT + MLA_V] = _dot(ckv, w_uv_ref[:, h * MLA_V:(h + 1) * MLA_V]).astype(BF16)
        v_mla_ref[:, h * V_EXT + MLA_V:(h + 1) * V_EXT] = ones

    g_q, g_k = g_q_gqa_ref[...], g_k_gqa_ref[...]
    for h in range(GQA_HEADS):
        qh = _dot(xb, w_in_ref[:, C_QG + h * GQA_HD:C_QG + (h + 1) * GQA_HD])
        qh = _rope(_rms(qh, g_q), cos_a, sin_a) * (GQA_SCALE * LOG2E)
        q_gqa_ref[:, h * GQA_HD:(h + 1) * GQA_HD] = qh.astype(BF16)
    for h in range(GQA_KV_HEADS):
        kh = _dot(xb, w_in_ref[:, C_KG + h * GQA_HD:C_KG + (h + 1) * GQA_HD])
        k_gqa_ref[:, h * GQA_HD:(h + 1) * GQA_HD] = _rope(_rms(kh, g_k), cos_a, sin_a).astype(BF16)
        vh = _dot(xb, w_in_ref[:, C_VG + h * GQA_HD:C_VG + (h + 1) * GQA_HD])
        v_gqa_ref[:, h * V_EXT:h * V_EXT + GQA_HD] = vh.astype(BF16)
        v_gqa_ref[:, h * V_EXT + GQA_HD:(h + 1) * V_EXT] = ones


def _proj(x, wl, tables, pos_block, tm):
    t = x.shape[0]
    row = lambda w: pl.BlockSpec((tm, w), lambda i: (i, 0))
    tab = pl.BlockSpec((tm, LANES), lambda i: (pos_block(i), 0))
    widths = (MLA_HEADS * MLA_QK_PAD, MLA_HEADS * MLA_QK_PAD, MLA_HEADS * V_EXT,
              GQA_WIDTH, GQA_KV_HEADS * GQA_HD, GQA_KV_HEADS * V_EXT)
    consts = (wl['w_in_p'], wl['w_uq_p'], wl['w_uk'], wl['w_uv'],
              wl['g_q_lat'], wl['g_kv_lat'], wl['g_q_gqa'], wl['g_k_gqa'])
    return pl.pallas_call(
        _proj_kernel,
        grid=(t // tm,),
        in_specs=[row(D_MODEL)] + [_const_spec(c.shape) for c in consts] + [tab] * 4,
        out_specs=[row(w) for w in widths],
        out_shape=[jax.ShapeDtypeStruct((t, w), BF16) for w in widths],
        compiler_params=_params(("arbitrary",)),
        name="proj",
    )(x, *consts, *tables)


def _attn_kernel(q_ref, k_ref, v_ref, o_ref, s_ref, p_ref, a_ref, m_ref, acc_ref, *, tk, dv):
    q = q_ref[...]
    nk = k_ref.shape[0] // tk

    def scores(c, slot):
        off = pl.multiple_of(c * tk, tk)
        s_ref[slot] = lax.dot_general(q, k_ref[pl.ds(off, tk), :], (((1,), (1,)), ((), ())),
                                      preferred_element_type=F32)

    def softmax(slot):
        s = s_ref[slot]
        m = m_ref[...]
        m_new = jnp.maximum(m, jnp.max(s, axis=1, keepdims=True))
        a_ref[slot] = jnp.exp2(m - m_new)
        p_ref[slot] = jnp.exp2(s - m_new).astype(BF16)
        m_ref[...] = m_new

    def values(c, slot):
        off = pl.multiple_of(c * tk, tk)
        acc_ref[...] = a_ref[slot] * acc_ref[...] + _dot(p_ref[slot], v_ref[pl.ds(off, tk), :])

    def tick(t, parity, first=False, last=False):
        if not last:
            scores(t + 1, 1 - parity)
        if not first:
            values(t - 1, 1 - parity)
        softmax(parity)

    m_ref[...] = jnp.full(m_ref.shape, -jnp.inf, F32)
    acc_ref[...] = jnp.zeros(acc_ref.shape, F32)
    scores(0, 0)
    if nk <= 4:
        for t in range(nk):
            tick(t, t % 2, first=t == 0, last=t == nk - 1)
    else:
        assert nk % 2 == 0
        tick(0, 0, first=True)
        n_loop = (nk - 2) // TICKS_PER_TRIP

        def trip(i, c):
            t = TICKS_PER_TRIP * i + 1
            for d in range(TICKS_PER_TRIP):
                tick(t + d, (1 + d) % 2)
            return c
        lax.fori_loop(0, n_loop, trip, 0)
        for t in range(1 + n_loop * TICKS_PER_TRIP, nk - 1):
            tick(t, t % 2)
        tick(nk - 1, 1, last=True)
    values(nk - 1, (nk - 1) % 2)
    acc = acc_ref[...]
    o_ref[...] = (acc[:, :dv] / acc[:, dv:]).astype(o_ref.dtype)


def _attention(q, k, v, *, row0, n_seq, seq, heads, kv_group, dq, dv, tq, tk, name):
    assert row0 % seq == 0 and seq % tq == 0 and seq % tk == 0 and V_EXT == 2 * dv
    nq = seq // tq
    q0, s0 = row0 // tq, row0 // seq
    return pl.pallas_call(
        functools.partial(_attn_kernel, tk=tk, dv=dv),
        grid=(n_seq, heads, nq),
        in_specs=[pl.BlockSpec((tq, dq), lambda b, h, i: (q0 + b * nq + i, h)),
                  pl.BlockSpec((seq, dq), lambda b, h, i: (s0 + b, h // kv_group)),
                  pl.BlockSpec((seq, V_EXT), lambda b, h, i: (s0 + b, h // kv_group))],
        out_specs=pl.BlockSpec((tq, dv), lambda b, h, i: (b * nq + i, h)),
        out_shape=jax.ShapeDtypeStruct((n_seq * seq, heads * dv), BF16),
        scratch_shapes=[pltpu.VMEM((2, tq, tk), F32), pltpu.VMEM((2, tq, tk), BF16),
                        pltpu.VMEM((2, tq, 1), F32), pltpu.VMEM((tq, 1), F32), pltpu.VMEM((tq, V_EXT), F32)],
        compiler_params=_params(("arbitrary", "arbitrary", "arbitrary")),
        name=name,
    )(q, k, v)


def _outproj_kernel(x_ref, om_ref, og_ref, g_om_ref, g_og_ref, w_o_ref, ln_g_ref, ln_b_ref,
                    x1_ref, x1p_ref):
    om = _rms(om_ref[...].astype(F32), g_om_ref[...]).astype(BF16)
    og = _rms(og_ref[...].astype(F32), g_og_ref[...]).astype(BF16)
    mix = _dot(om, w_o_ref[:MLA_WIDTH, :]) + _dot(og, w_o_ref[MLA_WIDTH:, :])
    x1 = _layer_norm(ALPHA * x_ref[...] + mix, ln_g_ref[...], ln_b_ref[...])
    x1_ref[...] = x1
    x1p_ref[...] = _pack_pair(x1[:, :HALF], x1[:, HALF:])


def _outproj(x, o_mla, o_gqa, wl, tm):
    t = x.shape[0]
    row = lambda w: pl.BlockSpec((tm, w), lambda i: (i, 0))
    consts = (wl['g_o_mla'], wl['g_o_gqa'], wl['w_o'], wl['ln1_g'], wl['ln1_b'])
    return pl.pallas_call(
        _outproj_kernel,
        grid=(t // tm,),
        in_specs=[row(D_MODEL), row(MLA_WIDTH), row(GQA_WIDTH)] + [_const_spec(c.shape) for c in consts],
        out_specs=[row(D_MODEL), row(HALF)],
        out_shape=[jax.ShapeDtypeStruct((t, D_MODEL), F32), jax.ShapeDtypeStruct((t, HALF), U32)],
        compiler_params=_params(("arbitrary",)),
        name="outproj",
    )(x, o_mla, o_gqa, *consts)


def _beats(a, ia, b, ib):
    return (a > b) | ((a == b) & (ia < ib))


def _route_kernel(x_ref, w_rt_ref, bias_ref, mask_ref, gate_ref, counts_ref):
    logits = lax.dot_general(w_rt_ref[...], x_ref[...], (((1,), (1,)), ((), ())),
                             precision=lax.Precision.HIGHEST, preferred_element_type=F32)
    scores = jax.nn.sigmoid(logits)
    sel = scores + bias_ref[...]
    tm = sel.shape[1]
    neg = jnp.float32(-jnp.inf)

    sub = lax.broadcasted_iota(I32, (EXPERTS_PER_GROUP, tm), 0)
    grp_rows = []
    for g in range(N_GROUPS):
        blk = sel[g * EXPERTS_PER_GROUP:(g + 1) * EXPERTS_PER_GROUP, :]
        m1 = jnp.max(blk, axis=0, keepdims=True)
        first = jnp.min(jnp.where(blk == m1, sub, EXPERTS_PER_GROUP), axis=0, keepdims=True)
        m2 = jnp.max(jnp.where(sub == first, neg, blk), axis=0, keepdims=True)
        grp_rows.append(m1 + m2)
    grp = jnp.concatenate(grp_rows, axis=0)

    gidx = lax.broadcasted_iota(I32, (N_GROUPS, tm), 0)
    grank = jnp.zeros((N_GROUPS, tm), I32)
    for g in range(N_GROUPS):
        grank += _beats(grp[g:g + 1, :], g, grp, gidx).astype(I32)
    gkeep = grank < TOPK_GROUPS
    keep = jnp.concatenate(
        [jnp.broadcast_to(gkeep[g:g + 1, :], (EXPERTS_PER_GROUP, tm)) for g in range(N_GROUPS)], axis=0)
    sel = jnp.where(keep, sel, neg)

    eidx = lax.broadcasted_iota(I32, (N_EXPERTS, tm), 0)
    erank = jnp.zeros((N_EXPERTS, tm), I32)
    for e in range(N_EXPERTS):
        erank += _beats(sel[e:e + 1, :], e, sel, eidx).astype(I32)
    chosen = (erank < TOP_K) & keep

    picked = jnp.where(chosen, scores, 0.0)
    gate = picked / jnp.sum(picked, axis=0, keepdims=True) * ROUTED_SCALE
    mask = chosen.astype(F32)
    mask_ref[...] = mask
    gate_ref[...] = gate

    @pl.when(pl.program_id(0) == 0)
    def _():
        counts_ref[...] = jnp.zeros_like(counts_ref)
    counts_ref[...] += jnp.broadcast_to(jnp.sum(mask, axis=1, keepdims=True), counts_ref.shape)


def _route(x1, w_rt, bias, tm):
    t = x1.shape[0]
    col = pl.BlockSpec((N_EXPERTS, tm), lambda i: (0, i))
    return pl.pallas_call(
        _route_kernel,
        grid=(t // tm,),
        in_specs=[pl.BlockSpec((tm, D_MODEL), lambda i: (i, 0)),
                  _const_spec(w_rt.shape), _const_spec(bias.shape)],
        out_specs=[col, col, pl.BlockSpec((N_EXPERTS, LANES), lambda i: (0, 0))],
        out_shape=[jax.ShapeDtypeStruct((N_EXPERTS, t), F32), jax.ShapeDtypeStruct((N_EXPERTS, t), F32),
                   jax.ShapeDtypeStruct((N_EXPERTS, LANES), F32)],
        compiler_params=_params(("arbitrary",)),
        name="route",
    )(x1, w_rt, bias)


def _slots_kernel(mask_ref, gate_ref, start_ref, slot_ref, w_ref, seen_ref):
    @pl.when(pl.program_id(0) == 0)
    def _():
        seen_ref[...] = jnp.zeros_like(seen_ref)

    mask = mask_ref[...]
    tm = mask.shape[1]
    mb = mask.astype(BF16)
    before = (lax.broadcasted_iota(I32, (tm, tm), 0) < lax.broadcasted_iota(I32, (tm, tm), 1)).astype(BF16)
    pos = _dot(mb, before)
    lower = (lax.broadcasted_iota(I32, (N_EXPERTS, N_EXPERTS), 1)
             < lax.broadcasted_iota(I32, (N_EXPERTS, N_EXPERTS), 0)).astype(BF16)
    rank = _dot(lower, mb)
    seen = seen_ref[:, 0:1]
    slot = start_ref[...] + seen + pos
    on = mask > 0.5
    gate = gate_ref[...]
    for k in range(TOP_K):
        pick = on & (rank == float(k))
        slot_ref[k:k + 1, :] = jnp.sum(jnp.where(pick, slot, 0.0), axis=0, keepdims=True).astype(I32)
        w_ref[k:k + 1, :] = jnp.sum(jnp.where(pick, gate, 0.0), axis=0, keepdims=True)
    seen_ref[...] += jnp.broadcast_to(jnp.sum(mask, axis=1, keepdims=True), seen_ref.shape)


def _slots(mask, gate, start, tm):
    t = mask.shape[1]
    col = pl.BlockSpec((N_EXPERTS, tm), lambda i: (0, i))
    out = pl.BlockSpec((TOP_K, tm), lambda i: (0, i))
    return pl.pallas_call(
        _slots_kernel,
        grid=(t // tm,),
        in_specs=[col, col, _const_spec(start.shape)],
        out_specs=[out, out],
        out_shape=[jax.ShapeDtypeStruct((TOP_K, t), I32), jax.ShapeDtypeStruct((TOP_K, t), F32)],
        scratch_shapes=[pltpu.VMEM((N_EXPERTS, LANES), F32)],
        compiler_params=_params(("arbitrary",)),
        name="slots",
    )(mask, gate, start)


def _dispatch_kernel(fill_start_ref, fill_n_ref, slot_ref, x_ref, y_hbm, zero_ref, sem, fill_sem, *, tm):
    def issue(t, c):
        for k in range(TOP_K):
            slot = slot_ref[0, 0, t * TOP_K + k]
            pltpu.make_async_copy(x_ref.at[pl.ds(t, 1)], y_hbm.at[pl.ds(slot, 1)], sem).start()
        return c
    lax.fori_loop(0, tm, issue, 0)

    @pl.when(pl.program_id(0) == 0)
    def _():
        zero_ref[...] = jnp.zeros_like(zero_ref)

        def fill_copy(r):
            return pltpu.make_async_copy(zero_ref, y_hbm.at[pl.ds(r, 1)], fill_sem)

        def per_expert(e, c):
            lax.fori_loop(0, fill_n_ref[e], lambda j, c2: (fill_copy(fill_start_ref[e] + j).start(), c2)[1], 0)
            lax.fori_loop(0, fill_n_ref[e], lambda j, c2: (fill_copy(fill_start_ref[e] + j).wait(), c2)[1], 0)
            return c
        lax.fori_loop(0, N_EXPERTS, per_expert, 0)

    rows = y_hbm.at[pl.ds(0, TOP_K * tm)]
    pltpu.make_async_copy(rows, rows, sem).wait()


def _dispatch(x1p, slot, fill_start, fill_n, n_slots, tm):
    t = x1p.shape[0]
    assert TOP_K * tm <= n_slots
    return pl.pallas_call(
        functools.partial(_dispatch_kernel, tm=tm),
        grid_spec=pltpu.PrefetchScalarGridSpec(
            num_scalar_prefetch=2,
            grid=(t // tm,),
            in_specs=[pl.BlockSpec((1, 1, tm * TOP_K), lambda i, *_: (i, 0, 0), memory_space=pltpu.SMEM),
                      pl.BlockSpec((tm, HALF), lambda i, *_: (i, 0))],
            out_specs=pl.BlockSpec(memory_space=pl.ANY),
            scratch_shapes=[pltpu.VMEM((1, HALF), U32), pltpu.SemaphoreType.DMA, pltpu.SemaphoreType.DMA]),
        out_shape=jax.ShapeDtypeStruct((n_slots, HALF), U32),
        compiler_params=pltpu.CompilerParams(dimension_semantics=("arbitrary",), has_side_effects=True),
        name="dispatch",
    )(fill_start, fill_n, slot, x1p)


def _experts_kernel(blk_e_ref, n_used_ref, y_ref, wg_ref, wu_ref, wd_ref, o_ref):
    @pl.when(pl.program_id(0) < n_used_ref[0])
    def _():
        lo, hi = _unpack_pair(y_ref[...])
        lo, hi = lo.astype(BF16), hi.astype(BF16)
        gate = _dot(lo, wg_ref[0, :HALF, :]) + _dot(hi, wg_ref[0, HALF:, :])
        up = _dot(lo, wu_ref[0, :HALF, :]) + _dot(hi, wu_ref[0, HALF:, :])
        hid = (jax.nn.silu(gate) * up).astype(BF16)
        out = _dot(hid, wd_ref[0])
        o_ref[...] = _pack_pair(out[:, :HALF], out[:, HALF:])


def _experts(y_in, blk_e, n_used, w_gate, w_up, w_down, bm):
    n_slots = y_in.shape[0]
    blk = lambda b, be, nu: (jnp.minimum(b, nu[0] - 1), 0)
    wsel = lambda b, be, nu: (be[b], 0, 0)
    return pl.pallas_call(
        _experts_kernel,
        grid_spec=pltpu.PrefetchScalarGridSpec(
            num_scalar_prefetch=2,
            grid=(n_slots // bm,),
            in_specs=[pl.BlockSpec((bm, HALF), blk),
                      pl.BlockSpec((1, D_MODEL, D_EXPERT), wsel),
                      pl.BlockSpec((1, D_MODEL, D_EXPERT), wsel),
                      pl.BlockSpec((1, D_EXPERT, D_MODEL), wsel)],
            out_specs=pl.BlockSpec((bm, HALF), blk)),
        out_shape=jax.ShapeDtypeStruct((n_slots, HALF), U32),
        compiler_params=_params(("arbitrary",)),
        name="experts",
    )(blk_e, n_used, y_in, w_gate, w_up, w_down)


def _final_kernel(slot_ref, slot_next_ref, y_hbm, w_ref, x1_ref, p_ref, ws_gate_ref, ws_up_ref, ws_down_ref,
                  w_pg_ref, w_ple_ref, ln_g_ref, ln_b_ref, o_ref, rows_ref, sem, *, tm):
    i = pl.program_id(0)
    cur = i % 2

    def gather(ids_ref, buf):
        def issue(t, c):
            for k in range(TOP_K):
                slot = ids_ref[0, 0, t * TOP_K + k]
                pltpu.make_async_copy(y_hbm.at[pl.ds(slot, 1)], rows_ref.at[buf, k, pl.ds(t, 1)],
                                      sem.at[buf]).start()
            return c
        lax.fori_loop(0, tm, issue, 0)

    @pl.when(i == 0)
    def _():
        gather(slot_ref, 0)

    has_next = i + 1 < pl.num_programs(0)
    for buf in range(2):
        @pl.when(has_next & (cur != buf))
        def _():
            gather(slot_next_ref, buf)

    x1 = x1_ref[...]
    xb = x1.astype(BF16)
    hid = (jax.nn.silu(_dot(xb, ws_gate_ref[...])) * _dot(xb, ws_up_ref[...])).astype(BF16)
    shared = _dot(hid, ws_down_ref[...])
    ple = jax.nn.sigmoid(_dot(xb, w_pg_ref[...])) * _dot(p_ref[...].astype(BF16), w_ple_ref[...])
    base = ALPHA * x1 + shared + ple

    pltpu.make_async_copy(rows_ref.at[cur], rows_ref.at[cur], sem.at[cur]).wait()

    w = w_ref[...]
    r_lo = jnp.zeros((tm, HALF), F32)
    r_hi = jnp.zeros((tm, HALF), F32)
    for k in range(TOP_K):
        lo, hi = _unpack_pair(rows_ref[cur, k])
        r_lo += w[:, k:k + 1] * lo
        r_hi += w[:, k:k + 1] * hi
    ln_g, ln_b = ln_g_ref[...], ln_b_ref[...]
    y = jnp.concatenate([base[:, :HALF] + r_lo, base[:, HALF:] + r_hi], axis=1)
    o_ref[...] = _layer_norm(y, ln_g, ln_b)


def _final(slot, y_out, w_tok, x1, p, wl, tm):
    t = x1.shape[0]
    n = t // tm
    row = lambda w: pl.BlockSpec((tm, w), lambda i: (i, 0))
    ids = lambda index: pl.BlockSpec((1, 1, tm * TOP_K), index, memory_space=pltpu.SMEM)
    consts = (wl['ws_gate'], wl['ws_up'], wl['ws_down'], wl['w_ple_gate'], wl['w_ple'], wl['ln2_g'], wl['ln2_b'])
    return pl.pallas_call(
        functools.partial(_final_kernel, tm=tm),
        grid=(n,),
        in_specs=[ids(lambda i: (i, 0, 0)), ids(lambda i: (jnp.minimum(i + 1, n - 1), 0, 0)),
                  pl.BlockSpec(memory_space=pl.ANY),
                  row(TOP_K), row(D_MODEL), row(PLE_DIM)] + [_const_spec(c.shape) for c in consts],
        out_specs=row(D_MODEL),
        out_shape=jax.ShapeDtypeStruct((t, D_MODEL), F32),
        scratch_shapes=[pltpu.VMEM((2, TOP_K, tm, HALF), U32), pltpu.SemaphoreType.DMA((2,))],
        compiler_params=_params(("arbitrary",)),
        name="final",
    )(slot, slot, y_out, w_tok, x1, p, *consts)


def _rope_tables(n):
    t = jnp.arange(n, dtype=F32)
    inv = ROPE_THETA ** (-jnp.arange(0, MLA_ROPE, 2, dtype=F32) / MLA_ROPE)
    a_seq = t[:, None] * inv[None, :]
    a_row = jnp.floor(t / GRID_W)[:, None] * inv[None, :]
    a_col = (t - jnp.floor(t / GRID_W) * GRID_W)[:, None] * inv[None, :]
    z = jnp.zeros((n, LANES // 2), F32)
    cos_m = jnp.concatenate([jnp.cos(a_seq), jnp.cos(a_seq), z], axis=1)
    sin_m = jnp.concatenate([-jnp.sin(a_seq), jnp.sin(a_seq), z], axis=1)
    cos_a = jnp.concatenate([jnp.cos(a_row), jnp.cos(a_row), jnp.cos(a_col), jnp.cos(a_col)], axis=1)
    sin_a = jnp.concatenate([-jnp.sin(a_row), jnp.sin(a_row), -jnp.sin(a_col), jnp.sin(a_col)], axis=1)
    return cos_m, sin_m, cos_a, sin_a


def _layer_weights(i, w_in, g_q_lat, g_kv_lat, w_uq, w_uk, w_uv, g_q_gqa, g_k_gqa, g_o_mla, g_o_gqa,
                   w_o, ln1_g, ln1_b, w_router, router_bias, w_gate, w_up, w_down, ws_gate, ws_up,
                   ws_down, w_ple_gate, w_ple, ln2_g, ln2_b):
    wi = w_in[i]
    kr0 = Q_LORA + KV_LORA
    w_in_p = jnp.concatenate(
        [wi[:, :kr0 + MLA_ROPE], jnp.zeros((D_MODEL, LANES - MLA_ROPE), F32), wi[:, kr0 + MLA_ROPE:]], axis=1)
    uq = w_uq[i].reshape(Q_LORA, MLA_HEADS, MLA_NOPE + MLA_ROPE)
    uq = jnp.pad(uq, ((0, 0), (0, 0), (0, MLA_QK_PAD - MLA_NOPE - MLA_ROPE)))
    vec = lambda a: a[i].reshape(1, -1).astype(F32)
    return dict(
        w_in_p=w_in_p.astype(BF16), w_uq_p=uq.reshape(Q_LORA, MLA_HEADS * MLA_QK_PAD).astype(BF16),
        w_uk=w_uk[i].astype(BF16), w_uv=w_uv[i].astype(BF16),
        g_q_lat=vec(g_q_lat), g_kv_lat=vec(g_kv_lat), g_q_gqa=vec(g_q_gqa), g_k_gqa=vec(g_k_gqa),
        g_o_mla=vec(g_o_mla), g_o_gqa=vec(g_o_gqa), w_o=w_o[i].astype(BF16),
        ln1_g=vec(ln1_g), ln1_b=vec(ln1_b),
        w_rt=w_router[i].T.astype(F32), bias=router_bias[i].reshape(-1, 1).astype(F32),
        w_gate=w_gate[i].astype(BF16), w_up=w_up[i].astype(BF16), w_down=w_down[i].astype(BF16),
        ws_gate=ws_gate[i].astype(BF16), ws_up=ws_up[i].astype(BF16), ws_down=ws_down[i].astype(BF16),
        w_ple_gate=w_ple_gate[i].astype(BF16), w_ple=w_ple[i].astype(BF16),
        ln2_g=vec(ln2_g), ln2_b=vec(ln2_b))


def _slot_plan(counts, bm, n_blocks):
    counts = counts.astype(I32)
    padded = (counts + bm - 1) // bm * bm
    pad_end = jnp.cumsum(padded)
    pad_start = pad_end - padded
    first_row = jnp.arange(n_blocks, dtype=I32) * bm
    blk_e = jnp.minimum(jnp.sum(pad_end[None, :] <= first_row[:, None], axis=1), N_EXPERTS - 1).astype(I32)
    n_used = jnp.maximum(pad_end[-1:] // bm, 1).astype(I32)
    return pad_start, pad_start + counts, padded - counts, blk_e, n_used


def kernel(x_prompt, x_sample, p_prompt, p_sample, w_in, g_q_lat, g_kv_lat, w_uq, w_uk, w_uv,
           g_q_gqa, g_k_gqa, g_o_mla, g_o_gqa, w_o, ln1_g, ln1_b, w_router, router_bias,
           w_gate, w_up, w_down, ws_gate, ws_up, ws_down, w_ple_gate, w_ple, ln2_g, ln2_b):
    weights = (w_in, g_q_lat, g_kv_lat, w_uq, w_uk, w_uv, g_q_gqa, g_k_gqa, g_o_mla, g_o_gqa, w_o,
               ln1_g, ln1_b, w_router, router_bias, w_gate, w_up, w_down, ws_gate, ws_up, ws_down,
               w_ple_gate, w_ple, ln2_g, ln2_b)
    b1, s1, _ = x_prompt.shape
    b2, s2, _ = x_sample.shape
    t1, t2 = b1 * s1, b2 * s2
    t = t1 + t2
    x = jnp.concatenate([x_prompt.reshape(t1, D_MODEL), x_sample.reshape(t2, D_MODEL)], axis=0)
    p = jnp.concatenate([p_prompt.reshape(DEPTH, t1, PLE_DIM), p_sample.reshape(DEPTH, t2, PLE_DIM)], axis=1)

    tm = _tile(min(s1, s2), 512)
    tm_final = _tile(min(s1, s2), 256)
    bm =_tile(t * TOP_K // N_EXPERTS, 512)
    n_blocks = t * TOP_K // bm + N_EXPERTS
    tables = _rope_tables(max(s1, s2))

    def pos_block(i):
        return jnp.where(i < t1 // tm, i % (s1 // tm), (i - t1 // tm) % (s2 // tm))

    for layer in range(DEPTH):
        wl = _layer_weights(layer, *weights)
        q_mla, k_mla, v_mla, q_gqa, k_gqa, v_gqa = _proj(x, wl, tables, pos_block, tm)
        groups = ((0, b1, s1), (t1, b2, s2))
        o_mla = jnp.concatenate([
            _attention(q_mla, k_mla, v_mla, row0=r0, n_seq=nb, seq=s, heads=MLA_HEADS, kv_group=1,
                       dq=MLA_QK_PAD, dv=MLA_V, tq=_tile(s, 512), tk=_tile(s, 2048), name="attn_mla")
            for r0, nb, s in groups], axis=0)
        o_gqa = jnp.concatenate([
            _attention(q_gqa, k_gqa, v_gqa, row0=r0, n_seq=nb, seq=s, heads=GQA_HEADS, kv_group=GQA_GROUP,
                       dq=GQA_HD, dv=GQA_HD, tq=_tile(s, 512), tk=_tile(s, 2048), name="attn_gqa")
            for r0, nb, s in groups], axis=0)
        x1, x1p = _outproj(x, o_mla, o_gqa, wl, tm)

        mask, gate, counts = _route(x1, wl['w_rt'], wl['bias'], tm)
        pad_start, fill_start, fill_n, blk_e, n_used = _slot_plan(counts[:, 0], bm, n_blocks)
        slot, w_k = _slots(mask, gate, pad_start.astype(F32).reshape(-1, 1), tm)
        slot_tok = slot.T
        y_in = _dispatch(x1p, slot_tok.reshape(t // tm, 1, tm * TOP_K), fill_start, fill_n, n_blocks * bm, tm)
        y_out = _experts(y_in, blk_e, n_used, wl['w_gate'], wl['w_up'], wl['w_down'], bm)
        x = _final(slot_tok.reshape(t // tm_final, 1, tm_final * TOP_K), y_out, w_k.T, x1, p[layer], wl, tm_final)

    return x[:t1].reshape(b1, s1, D_MODEL), x[t1:].reshape(b2, s2, D_MODEL)
```

```python
import functools

import jax
import jax.numpy as jnp
from jax import lax
from jax.experimental import pallas as pl
from jax.experimental.pallas import tpu as pltpu

F32 = jnp.float32
BF16 = jnp.bfloat16
U32 = jnp.uint32
I32 = jnp.int32

D_MODEL = 2048
DEPTH = 2
GRID_W = 64
ROPE_THETA = 10000.0
MLA_HEADS = 8
MLA_NOPE = 128
MLA_ROPE = 64
MLA_V = 128
MLA_QK_PAD = 256
Q_LORA = 512
KV_LORA = 512
MLA_WIDTH = MLA_HEADS * MLA_V
MLA_SCALE = (MLA_NOPE + MLA_ROPE) ** -0.5
GQA_HD = 128
GQA_HEADS = 8
GQA_KV_HEADS = 2
GQA_GROUP = GQA_HEADS // GQA_KV_HEADS
GQA_WIDTH = GQA_HEADS * GQA_HD
GQA_SCALE = GQA_HD ** -0.5
N_EXPERTS = 64
TOP_K = 8
N_GROUPS = 8
TOPK_GROUPS = 4
EXPERTS_PER_GROUP = N_EXPERTS // N_GROUPS
D_EXPERT = 512
ROUTED_SCALE = 2.5
PLE_DIM = 256
ALPHA = (2 * DEPTH) ** 0.25
RMS_EPS = 1e-6
LN_EPS = 1e-5
V_EXT = 256
LOG2E = 1.4426950408889634
TICKS_PER_TRIP = 2
HALF = D_MODEL // 2

LANES = 128
VMEM_LIMIT = 56 * 1024 * 1024

C_CQ = 0
C_CKV = C_CQ + Q_LORA
C_KR = C_CKV + KV_LORA
C_QG = C_KR + LANES
C_KG = C_QG + GQA_WIDTH
C_VG = C_KG + GQA_KV_HEADS * GQA_HD
IN_COLS_PAD = C_VG + GQA_KV_HEADS * GQA_HD


def _tile(n, pref):
    t = min(n, pref)
    while n % t:
        t //= 2
    return t


def _const_spec(shape):
    nd = len(shape)
    return pl.BlockSpec(shape, lambda *_: (0,) * nd, pipeline_mode=pl.Buffered(1))


def _params(sem):
    return pltpu.CompilerParams(dimension_semantics=sem, vmem_limit_bytes=VMEM_LIMIT)


def _pack_pair(lo, hi):
    ulo = lax.bitcast_convert_type(lo.astype(BF16).astype(F32), U32)
    uhi = lax.bitcast_convert_type(hi.astype(BF16).astype(F32), U32)
    return (ulo >> 16) | uhi


def _unpack_pair(u):
    lo = lax.bitcast_convert_type(u << 16, F32)
    hi = lax.bitcast_convert_type(u & jnp.uint32(0xFFFF0000), F32)
    return lo, hi


def _rot_half_partner(x):
    lane = lax.broadcasted_iota(I32, x.shape, x.ndim - 1)
    n = x.shape[-1]
    up = pltpu.roll(x, n - 32, x.ndim - 1)
    down = pltpu.roll(x, 32, x.ndim - 1)
    return jnp.where((lane & 32) == 0, up, down)


def _rope(x, cos, sin_signed):
    return x * cos + _rot_half_partner(x) * sin_signed


def _rms(x, g):
    return x * lax.rsqrt(jnp.mean(x * x, axis=-1, keepdims=True) + RMS_EPS) * g


def _layer_norm(x, g, b):
    mu = jnp.mean(x, axis=-1, keepdims=True)
    xc = x - mu
    var = jnp.mean(xc * xc, axis=-1, keepdims=True)
    return xc * lax.rsqrt(var + LN_EPS) * g + b


def _dot(a, b):
    return jnp.dot(a, b, preferred_element_type=F32)


def _proj_kernel(x_ref, w_in_ref, w_uq_ref, w_uk_ref, w_uv_ref, g_q_lat_ref, g_kv_lat_ref,
                 g_q_gqa_ref, g_k_gqa_ref, cos_m_ref, sin_m_ref, cos_a_ref, sin_a_ref,
                 q_mla_ref, k_mla_ref, v_mla_ref, q_gqa_ref, k_gqa_ref, v_gqa_ref):
    xb = x_ref[...].astype(BF16)
    cos_m, sin_m = cos_m_ref[...], sin_m_ref[...]
    cos_a, sin_a = cos_a_ref[...], sin_a_ref[...]

    lat = _dot(xb, w_in_ref[:, C_CQ:C_QG])

    cq = _rms(lat[:, C_CQ:C_CQ + Q_LORA], g_q_lat_ref[...]).astype(BF16)
    q_all = _dot(cq, w_uq_ref[...])
    for h in range(MLA_HEADS):
        c0, c1 = h * MLA_QK_PAD, h * MLA_QK_PAD + MLA_NOPE
        q_rope = _rope(q_all[:, c1:c0 + MLA_QK_PAD], cos_m, sin_m)
        q_mla_ref[:, c0:c1] = (q_all[:, c0:c1] * (MLA_SCALE * LOG2E)).astype(BF16)
        q_mla_ref[:, c1:c0 + MLA_QK_PAD] = (q_rope * (MLA_SCALE * LOG2E)).astype(BF16)

    ckv = _rms(lat[:, C_CKV:C_CKV + KV_LORA], g_kv_lat_ref[...]).astype(BF16)
    k_rope = _rope(lat[:, C_KR:C_KR + LANES], cos_m, sin_m).astype(BF16)
    k_all = _dot(ckv, w_uk_ref[...]).astype(BF16)
    v_all = _dot(ckv, w_uv_ref[...]).astype(BF16)
    ones = jnp.ones((xb.shape[0], V_EXT - MLA_V), BF16)
    for h in range(MLA_HEADS):
        c0 = h * MLA_QK_PAD
        k_mla_ref[:, c0:c0 + MLA_NOPE] = k_all[:, h * MLA_NOPE:(h + 1) * MLA_NOPE]
        k_mla_ref[:, c0 + MLA_NOPE:c0 + MLA_QK_PAD] = k_rope
        v_mla_ref[:, h * V_EXT:h * V_EXT + MLA_V] = v_all[:, h * MLA_V:(h + 1) * MLA_V]
        v_mla_ref[:, h * V_EXT + MLA_V:(h + 1) * V_EXT] = ones

    g_q, g_k = g_q_gqa_ref[...], g_k_gqa_ref[...]
    qg_all = _dot(xb, w_in_ref[:, C_QG:C_KG])
    for h in range(GQA_HEADS):
        qh = _rope(_rms(qg_all[:, h * GQA_HD:(h + 1) * GQA_HD], g_q), cos_a, sin_a) * (GQA_SCALE * LOG2E)
        q_gqa_ref[:, h * GQA_HD:(h + 1) * GQA_HD] = qh.astype(BF16)
    kv_all = _dot(xb, w_in_ref[:, C_KG:IN_COLS_PAD])
    for h in range(GQA_KV_HEADS):
        kh = kv_all[:, h * GQA_HD:(h + 1) * GQA_HD]
        k_gqa_ref[:, h * GQA_HD:(h + 1) * GQA_HD] = _rope(_rms(kh, g_k), cos_a, sin_a).astype(BF16)
        vh = kv_all[:, C_VG - C_KG + h * GQA_HD:C_VG - C_KG + (h + 1) * GQA_HD]
        v_gqa_ref[:, h * V_EXT:h * V_EXT + GQA_HD] = vh.astype(BF16)
        v_gqa_ref[:, h * V_EXT + GQA_HD:(h + 1) * V_EXT] = ones


def _proj(x, wl, tables, pos_block, tm):
    t = x.shape[0]
    row = lambda w: pl.BlockSpec((tm, w), lambda i: (i, 0))
    tab = pl.BlockSpec((tm, LANES), lambda i: (pos_block(i), 0))
    widths = (MLA_HEADS * MLA_QK_PAD, MLA_HEADS * MLA_QK_PAD, MLA_HEADS * V_EXT,
              GQA_WIDTH, GQA_KV_HEADS * GQA_HD, GQA_KV_HEADS * V_EXT)
    consts = (wl['w_in_p'], wl['w_uq_p'], wl['w_uk'], wl['w_uv'],
              wl['g_q_lat'], wl['g_kv_lat'], wl['g_q_gqa'], wl['g_k_gqa'])
    return pl.pallas_call(
        _proj_kernel,
        grid=(t // tm,),
        in_specs=[row(D_MODEL)] + [_const_spec(c.shape) for c in consts] + [tab] * 4,
        out_specs=[row(w) for w in widths],
        out_shape=[jax.ShapeDtypeStruct((t, w), BF16) for w in widths],
        compiler_params=_params(("arbitrary",)),
        name="proj",
    )(x, *consts, *tables)


def _attn_kernel(q_ref, k_ref, v_ref, o_ref, s_ref, p_ref, a_ref, m_ref, acc_ref, *, tk, dv):
    q = q_ref[...]
    nk = k_ref.shape[0] // tk

    def scores(c, slot):
        off = pl.multiple_of(c * tk, tk)
        s_ref[slot] = lax.dot_general(q, k_ref[pl.ds(off, tk), :], (((1,), (1,)), ((), ())),
                                      preferred_element_type=F32)

    def softmax(slot):
        s = s_ref[slot]
        m = m_ref[...]
        m_new = jnp.maximum(m, jnp.max(s, axis=1, keepdims=True))
        a_ref[slot] = jnp.exp2(m - m_new)
        p_ref[slot] = jnp.exp2(s - m_new).astype(BF16)
        m_ref[...] = m_new

    def values(c, slot):
        off = pl.multiple_of(c * tk, tk)
        acc_ref[...] = a_ref[slot] * acc_ref[...] + _dot(p_ref[slot], v_ref[pl.ds(off, tk), :])

    def tick(t, parity, first=False, last=False):
        if not last:
            scores(t + 1, 1 - parity)
        if not first:
            values(t - 1, 1 - parity)
        softmax(parity)

    m_ref[...] = jnp.full(m_ref.shape, -jnp.inf, F32)
    acc_ref[...] = jnp.zeros(acc_ref.shape, F32)
    scores(0, 0)
    if nk <= 4:
        for t in range(nk):
            tick(t, t % 2, first=t == 0, last=t == nk - 1)
    else:
        assert nk % 2 == 0
        tick(0, 0, first=True)
        n_loop = (nk - 2) // TICKS_PER_TRIP

        def trip(i, c):
            t = TICKS_PER_TRIP * i + 1
            for d in range(TICKS_PER_TRIP):
                tick(t + d, (1 + d) % 2)
            return c
        lax.fori_loop(0, n_loop, trip, 0)
        for t in range(1 + n_loop * TICKS_PER_TRIP, nk - 1):
            tick(t, t % 2)
        tick(nk - 1, 1, last=True)
    values(nk - 1, (nk - 1) % 2)
    acc = acc_ref[...]
    o_ref[...] = (acc[:, :dv] / acc[:, dv:]).astype(o_ref.dtype)


def _attention(q, k, v, *, row0, n_seq, seq, heads, kv_group, dq, dv, tq, tk, name):
    assert row0 % seq == 0 and seq % tq == 0 and seq % tk == 0 and V_EXT == 2 * dv
    nq = seq // tq
    q0, s0 = row0 // tq, row0 // seq
    return pl.pallas_call(
        functools.partial(_attn_kernel, tk=tk, dv=dv),
        grid=(n_seq, heads, nq),
        in_specs=[pl.BlockSpec((tq, dq), lambda b, h, i: (q0 + b * nq + i, h)),
                  pl.BlockSpec((seq, dq), lambda b, h, i: (s0 + b, h // kv_group)),
                  pl.BlockSpec((seq, V_EXT), lambda b, h, i: (s0 + b, h // kv_group))],
        out_specs=pl.BlockSpec((tq, dv), lambda b, h, i: (b * nq + i, h)),
        out_shape=jax.ShapeDtypeStruct((n_seq * seq, heads * dv), BF16),
        scratch_shapes=[pltpu.VMEM((2, tq, tk), F32), pltpu.VMEM((2, tq, tk), BF16),
                        pltpu.VMEM((2, tq, 1), F32), pltpu.VMEM((tq, 1), F32), pltpu.VMEM((tq, V_EXT), F32)],
        compiler_params=_params(("arbitrary", "arbitrary", "arbitrary")),
        name=name,
    )(q, k, v)


def _outproj_kernel(x_ref, om_ref, og_ref, g_om_ref, g_og_ref, w_o_ref, ln_g_ref, ln_b_ref,
                    x1_ref, x1p_ref):
    om = _rms(om_ref[...].astype(F32), g_om_ref[...]).astype(BF16)
    og = _rms(og_ref[...].astype(F32), g_og_ref[...]).astype(BF16)
    mix = _dot(om, w_o_ref[:MLA_WIDTH, :]) + _dot(og, w_o_ref[MLA_WIDTH:, :])
    x1 = _layer_norm(ALPHA * x_ref[...] + mix, ln_g_ref[...], ln_b_ref[...])
    x1_ref[...] = x1
    x1p_ref[...] = _pack_pair(x1[:, :HALF], x1[:, HALF:])


def _outproj(x, o_mla, o_gqa, wl, tm):
    t = x.shape[0]
    row = lambda w: pl.BlockSpec((tm, w), lambda i: (i, 0))
    consts = (wl['g_o_mla'], wl['g_o_gqa'], wl['w_o'], wl['ln1_g'], wl['ln1_b'])
    return pl.pallas_call(
        _outproj_kernel,
        grid=(t // tm,),
        in_specs=[row(D_MODEL), row(MLA_WIDTH), row(GQA_WIDTH)] + [_const_spec(c.shape) for c in consts],
        out_specs=[row(D_MODEL), row(HALF)],
        out_shape=[jax.ShapeDtypeStruct((t, D_MODEL), F32), jax.ShapeDtypeStruct((t, HALF), U32)],
        compiler_params=_params(("arbitrary",)),
        name="outproj",
    )(x, o_mla, o_gqa, *consts)


def _beats(a, ia, b, ib):
    return (a > b) | ((a == b) & (ia < ib))


def _route_kernel(x_ref, w_rt_ref, bias_ref, mask_ref, gate_ref, counts_ref):
    logits = lax.dot_general(w_rt_ref[...], x_ref[...], (((1,), (1,)), ((), ())),
                             precision=lax.Precision.HIGHEST, preferred_element_type=F32)
    scores = jax.nn.sigmoid(logits)
    sel = scores + bias_ref[...]
    tm = sel.shape[1]
    neg = jnp.float32(-jnp.inf)

    sub = lax.broadcasted_iota(I32, (EXPERTS_PER_GROUP, tm), 0)
    grp_rows = []
    for g in range(N_GROUPS):
        blk = sel[g * EXPERTS_PER_GROUP:(g + 1) * EXPERTS_PER_GROUP, :]
        m1 = jnp.max(blk, axis=0, keepdims=True)
        first = jnp.min(jnp.where(blk == m1, sub, EXPERTS_PER_GROUP), axis=0, keepdims=True)
        m2 = jnp.max(jnp.where(sub == first, neg, blk), axis=0, keepdims=True)
        grp_rows.append(m1 + m2)
    grp = jnp.concatenate(grp_rows, axis=0)

    gidx = lax.broadcasted_iota(I32, (N_GROUPS, tm), 0)
    grank = jnp.zeros((N_GROUPS, tm), I32)
    for g in range(N_GROUPS):
        grank += _beats(grp[g:g + 1, :], g, grp, gidx).astype(I32)
    gkeep = grank < TOPK_GROUPS
    keep = jnp.concatenate(
        [jnp.broadcast_to(gkeep[g:g + 1, :], (EXPERTS_PER_GROUP, tm)) for g in range(N_GROUPS)], axis=0)
    sel = jnp.where(keep, sel, neg)

    eidx = lax.broadcasted_iota(I32, (N_EXPERTS, tm), 0)
    erank = jnp.zeros((N_EXPERTS, tm), I32)
    for e in range(N_EXPERTS):
        erank += _beats(sel[e:e + 1, :], e, sel, eidx).astype(I32)
    chosen = (erank < TOP_K) & keep

    picked = jnp.where(chosen, scores, 0.0)
    gate = picked / jnp.sum(picked, axis=0, keepdims=True) * ROUTED_SCALE
    mask = chosen.astype(F32)
    mask_ref[...] = mask
    gate_ref[...] = gate

    @pl.when(pl.program_id(0) == 0)
    def _():
        counts_ref[...] = jnp.zeros_like(counts_ref)
    counts_ref[...] += jnp.broadcast_to(jnp.sum(mask, axis=1, keepdims=True), counts_ref.shape)


def _route(x1, w_rt, bias, tm):
    t = x1.shape[0]
    col = pl.BlockSpec((N_EXPERTS, tm), lambda i: (0, i))
    return pl.pallas_call(
        _route_kernel,
        grid=(t // tm,),
        in_specs=[pl.BlockSpec((tm, D_MODEL), lambda i: (i, 0)),
                  _const_spec(w_rt.shape), _const_spec(bias.shape)],
        out_specs=[col, col, pl.BlockSpec((N_EXPERTS, LANES), lambda i: (0, 0))],
        out_shape=[jax.ShapeDtypeStruct((N_EXPERTS, t), F32), jax.ShapeDtypeStruct((N_EXPERTS, t), F32),
                   jax.ShapeDtypeStruct((N_EXPERTS, LANES), F32)],
        compiler_params=_params(("arbitrary",)),
        name="route",
    )(x1, w_rt, bias)


def _slots_kernel(mask_ref, gate_ref, start_ref, slot_ref, w_ref, seen_ref):
    @pl.when(pl.program_id(0) == 0)
    def _():
        seen_ref[...] = jnp.zeros_like(seen_ref)

    mask = mask_ref[...]
    tm = mask.shape[1]
    mb = mask.astype(BF16)
    before = (lax.broadcasted_iota(I32, (tm, tm), 0) < lax.broadcasted_iota(I32, (tm, tm), 1)).astype(BF16)
    pos = _dot(mb, before)
    lower = (lax.broadcasted_iota(I32, (N_EXPERTS, N_EXPERTS), 1)
             < lax.broadcasted_iota(I32, (N_EXPERTS, N_EXPERTS), 0)).astype(BF16)
    rank = _dot(lower, mb)
    seen = seen_ref[:, 0:1]
    slot = start_ref[...] + seen + pos
    on = mask > 0.5
    gate = gate_ref[...]
    for k in range(TOP_K):
        pick = on & (rank == float(k))
        slot_ref[k:k + 1, :] = jnp.sum(jnp.where(pick, slot, 0.0), axis=0, keepdims=True).astype(I32)
        w_ref[k:k + 1, :] = jnp.sum(jnp.where(pick, gate, 0.0), axis=0, keepdims=True)
    seen_ref[...] += jnp.broadcast_to(jnp.sum(mask, axis=1, keepdims=True), seen_ref.shape)


def _slots(mask, gate, start, tm):
    t = mask.shape[1]
    col = pl.BlockSpec((N_EXPERTS, tm), lambda i: (0, i))
    out = pl.BlockSpec((TOP_K, tm), lambda i: (0, i))
    return pl.pallas_call(
        _slots_kernel,
        grid=(t // tm,),
        in_specs=[col, col, _const_spec(start.shape)],
        out_specs=[out, out],
        out_shape=[jax.ShapeDtypeStruct((TOP_K, t), I32), jax.ShapeDtypeStruct((TOP_K, t), F32)],
        scratch_shapes=[pltpu.VMEM((N_EXPERTS, LANES), F32)],
        compiler_params=_params(("arbitrary",)),
        name="slots",
    )(mask, gate, start)


def _dispatch_kernel(fill_start_ref, fill_n_ref, slot_ref, x_ref, y_hbm, zero_ref, sem, fill_sem, *, tm):
    def issue(t, c):
        for k in range(TOP_K):
            slot = slot_ref[0, 0, t * TOP_K + k]
            pltpu.make_async_copy(x_ref.at[pl.ds(t, 1)], y_hbm.at[pl.ds(slot, 1)], sem).start()
        return c
    lax.fori_loop(0, tm, issue, 0)

    @pl.when(pl.program_id(0) == 0)
    def _():
        zero_ref[...] = jnp.zeros_like(zero_ref)

        def fill_copy(r):
            return pltpu.make_async_copy(zero_ref, y_hbm.at[pl.ds(r, 1)], fill_sem)

        def per_expert(e, c):
            lax.fori_loop(0, fill_n_ref[e], lambda j, c2: (fill_copy(fill_start_ref[e] + j).start(), c2)[1], 0)
            lax.fori_loop(0, fill_n_ref[e], lambda j, c2: (fill_copy(fill_start_ref[e] + j).wait(), c2)[1], 0)
            return c
        lax.fori_loop(0, N_EXPERTS, per_expert, 0)

    rows = y_hbm.at[pl.ds(0, TOP_K * tm)]
    pltpu.make_async_copy(rows, rows, sem).wait()


def _dispatch(x1p, slot, fill_start, fill_n, n_slots, tm):
    t = x1p.shape[0]
    assert TOP_K * tm <= n_slots
    return pl.pallas_call(
        functools.partial(_dispatch_kernel, tm=tm),
        grid_spec=pltpu.PrefetchScalarGridSpec(
            num_scalar_prefetch=2,
            grid=(t // tm,),
            in_specs=[pl.BlockSpec((1, 1, tm * TOP_K), lambda i, *_: (i, 0, 0), memory_space=pltpu.SMEM),
                      pl.BlockSpec((tm, HALF), lambda i, *_: (i, 0))],
            out_specs=pl.BlockSpec(memory_space=pl.ANY),
            scratch_shapes=[pltpu.VMEM((1, HALF), U32), pltpu.SemaphoreType.DMA, pltpu.SemaphoreType.DMA]),
        out_shape=jax.ShapeDtypeStruct((n_slots, HALF), U32),
        compiler_params=pltpu.CompilerParams(dimension_semantics=("arbitrary",), has_side_effects=True),
        name="dispatch",
    )(fill_start, fill_n, slot, x1p)


def _experts_kernel(blk_e_ref, n_used_ref, y_ref, wg_ref, wu_ref, wd_ref, o_ref, wg_s, wu_s, wd_s):
    b = pl.program_id(0)

    @pl.when(b < n_used_ref[0])
    def _():
        @pl.when((b == 0) | (blk_e_ref[b] != blk_e_ref[jnp.maximum(b - 1, 0)]))
        def _():
            wg_s[...] = wg_ref[0, 0].astype(BF16)
            wu_s[...] = wu_ref[0, 0].astype(BF16)
            wd_s[...] = wd_ref[0, 0].astype(BF16)

        lo, hi = _unpack_pair(y_ref[...])
        lo, hi = lo.astype(BF16), hi.astype(BF16)
        gate = _dot(lo, wg_s[:HALF, :]) + _dot(hi, wg_s[HALF:, :])
        up = _dot(lo, wu_s[:HALF, :]) + _dot(hi, wu_s[HALF:, :])
        hid = (jax.nn.silu(gate) * up).astype(BF16)
        out = _dot(hid, wd_s[...])
        o_ref[...] = _pack_pair(out[:, :HALF], out[:, HALF:])


def _experts(y_in, blk_e, n_used, w_gate, w_up, w_down, layer, bm):
    n_slots = y_in.shape[0]
    blk = lambda b, be, nu: (jnp.minimum(b, nu[0] - 1), 0)
    wsel = lambda b, be, nu: (layer, be[b], 0, 0)
    return pl.pallas_call(
        _experts_kernel,
        grid_spec=pltpu.PrefetchScalarGridSpec(
            num_scalar_prefetch=2,
            grid=(n_slots // bm,),
            in_specs=[pl.BlockSpec((bm, HALF), blk),
                      pl.BlockSpec((1, 1, D_MODEL, D_EXPERT), wsel),
                      pl.BlockSpec((1, 1, D_MODEL, D_EXPERT), wsel),
                      pl.BlockSpec((1, 1, D_EXPERT, D_MODEL), wsel)],
            out_specs=pl.BlockSpec((bm, HALF), blk),
            scratch_shapes=[pltpu.VMEM((D_MODEL, D_EXPERT), BF16), pltpu.VMEM((D_MODEL, D_EXPERT), BF16),
                            pltpu.VMEM((D_EXPERT, D_MODEL), BF16)]),
        out_shape=jax.ShapeDtypeStruct((n_slots, HALF), U32),
        compiler_params=_params(("arbitrary",)),
        name="experts",
    )(blk_e, n_used, y_in, w_gate, w_up, w_down)


def _final_kernel(slot_ref, slot_next_ref, y_hbm, w_ref, x1_ref, p_ref, ws_gate_ref, ws_up_ref, ws_down_ref,
                  w_pg_ref, w_ple_ref, ln_g_ref, ln_b_ref, o_ref, rows_ref, sem, *, tm):
    i = pl.program_id(0)
    cur = i % 2

    def gather(ids_ref, buf):
        def issue(t, c):
            for k in range(TOP_K):
                slot = ids_ref[0, 0, t * TOP_K + k]
                pltpu.make_async_copy(y_hbm.at[pl.ds(slot, 1)], rows_ref.at[buf, k, pl.ds(t, 1)],
                                      sem.at[buf]).start()
            return c
        lax.fori_loop(0, tm, issue, 0)

    @pl.when(i == 0)
    def _():
        gather(slot_ref, 0)

    has_next = i + 1 < pl.num_programs(0)
    for buf in range(2):
        @pl.when(has_next & (cur != buf))
        def _():
            gather(slot_next_ref, buf)

    x1 = x1_ref[...]
    xb = x1.astype(BF16)
    hid = (jax.nn.silu(_dot(xb, ws_gate_ref[...])) * _dot(xb, ws_up_ref[...])).astype(BF16)
    shared = _dot(hid, ws_down_ref[...])
    ple = jax.nn.sigmoid(_dot(xb, w_pg_ref[...])) * _dot(p_ref[...].astype(BF16), w_ple_ref[...])
    base = ALPHA * x1 + shared + ple

    pltpu.make_async_copy(rows_ref.at[cur], rows_ref.at[cur], sem.at[cur]).wait()

    w = w_ref[...]
    r_lo = jnp.zeros((tm, HALF), F32)
    r_hi = jnp.zeros((tm, HALF), F32)
    for k in range(TOP_K):
        lo, hi = _unpack_pair(rows_ref[cur, k])
        r_lo += w[:, k:k + 1] * lo
        r_hi += w[:, k:k + 1] * hi
    ln_g, ln_b = ln_g_ref[...], ln_b_ref[...]
    y = jnp.concatenate([base[:, :HALF] + r_lo, base[:, HALF:] + r_hi], axis=1)
    o_ref[...] = _layer_norm(y, ln_g, ln_b)


def _final(slot, y_out, w_tok, x1, p, wl, tm, tile0, n):
    row = lambda w: pl.BlockSpec((tm, w), lambda i: (tile0 + i, 0))
    ids = lambda index: pl.BlockSpec((1, 1, tm * TOP_K), index, memory_space=pltpu.SMEM)
    consts = (wl['ws_gate'], wl['ws_up'], wl['ws_down'], wl['w_ple_gate'], wl['w_ple'], wl['ln2_g'], wl['ln2_b'])
    return pl.pallas_call(
        functools.partial(_final_kernel, tm=tm),
        grid=(n,),
        in_specs=[ids(lambda i: (tile0 + i, 0, 0)), ids(lambda i: (tile0 + jnp.minimum(i + 1, n - 1), 0, 0)),
                  pl.BlockSpec(memory_space=pl.ANY),
                  row(TOP_K), row(D_MODEL), row(PLE_DIM)] + [_const_spec(c.shape) for c in consts],
        out_specs=pl.BlockSpec((tm, D_MODEL), lambda i: (i, 0)),
        out_shape=jax.ShapeDtypeStruct((n * tm, D_MODEL), F32),
        scratch_shapes=[pltpu.VMEM((2, TOP_K, tm, HALF), U32), pltpu.SemaphoreType.DMA((2,))],
        compiler_params=_params(("arbitrary",)),
        name="final",
    )(slot, slot, y_out, w_tok, x1, p, *consts)


def _rope_tables(n):
    t = jnp.arange(n, dtype=F32)
    inv = ROPE_THETA ** (-jnp.arange(0, MLA_ROPE, 2, dtype=F32) / MLA_ROPE)
    a_seq = t[:, None] * inv[None, :]
    a_row = jnp.floor(t / GRID_W)[:, None] * inv[None, :]
    a_col = (t - jnp.floor(t / GRID_W) * GRID_W)[:, None] * inv[None, :]
    z = jnp.zeros((n, LANES // 2), F32)
    cos_m = jnp.concatenate([jnp.cos(a_seq), jnp.cos(a_seq), z], axis=1)
    sin_m = jnp.concatenate([-jnp.sin(a_seq), jnp.sin(a_seq), z], axis=1)
    cos_a = jnp.concatenate([jnp.cos(a_row), jnp.cos(a_row), jnp.cos(a_col), jnp.cos(a_col)], axis=1)
    sin_a = jnp.concatenate([-jnp.sin(a_row), jnp.sin(a_row), -jnp.sin(a_col), jnp.sin(a_col)], axis=1)
    return cos_m, sin_m, cos_a, sin_a


def _layer_weights(i, w_in, g_q_lat, g_kv_lat, w_uq, w_uk, w_uv, g_q_gqa, g_k_gqa, g_o_mla, g_o_gqa,
                   w_o, ln1_g, ln1_b, w_router, router_bias, w_gate, w_up, w_down, ws_gate, ws_up,
                   ws_down, w_ple_gate, w_ple, ln2_g, ln2_b):
    wi = w_in[i]
    kr0 = Q_LORA + KV_LORA
    w_in_p = jnp.concatenate(
        [wi[:, :kr0 + MLA_ROPE], jnp.zeros((D_MODEL, LANES - MLA_ROPE), F32), wi[:, kr0 + MLA_ROPE:]], axis=1)
    uq = w_uq[i].reshape(Q_LORA, MLA_HEADS, MLA_NOPE + MLA_ROPE)
    uq = jnp.pad(uq, ((0, 0), (0, 0), (0, MLA_QK_PAD - MLA_NOPE - MLA_ROPE)))
    vec = lambda a: a[i].reshape(1, -1).astype(F32)
    return dict(
        w_in_p=w_in_p.astype(BF16), w_uq_p=uq.reshape(Q_LORA, MLA_HEADS * MLA_QK_PAD).astype(BF16),
        w_uk=w_uk[i].astype(BF16), w_uv=w_uv[i].astype(BF16),
        g_q_lat=vec(g_q_lat), g_kv_lat=vec(g_kv_lat), g_q_gqa=vec(g_q_gqa), g_k_gqa=vec(g_k_gqa),
        g_o_mla=vec(g_o_mla), g_o_gqa=vec(g_o_gqa), w_o=w_o[i].astype(BF16),
        ln1_g=vec(ln1_g), ln1_b=vec(ln1_b),
        w_rt=w_router[i].T.astype(F32), bias=router_bias[i].reshape(-1, 1).astype(F32),
        ws_gate=ws_gate[i].astype(BF16), ws_up=ws_up[i].astype(BF16), ws_down=ws_down[i].astype(BF16),
        w_ple_gate=w_ple_gate[i].astype(BF16), w_ple=w_ple[i].astype(BF16),
        ln2_g=vec(ln2_g), ln2_b=vec(ln2_b))


def _slot_plan(counts, bm, n_blocks):
    counts = counts.astype(I32)
    padded = (counts + bm - 1) // bm * bm
    pad_end = jnp.cumsum(padded)
    pad_start = pad_end - padded
    first_row = jnp.arange(n_blocks, dtype=I32) * bm
    blk_e = jnp.minimum(jnp.sum(pad_end[None, :] <= first_row[:, None], axis=1), N_EXPERTS - 1).astype(I32)
    n_used = jnp.maximum(pad_end[-1:] // bm, 1).astype(I32)
    return pad_start, pad_start + counts, padded - counts, blk_e, n_used


def kernel(x_prompt, x_sample, p_prompt, p_sample, w_in, g_q_lat, g_kv_lat, w_uq, w_uk, w_uv,
           g_q_gqa, g_k_gqa, g_o_mla, g_o_gqa, w_o, ln1_g, ln1_b, w_router, router_bias,
           w_gate, w_up, w_down, ws_gate, ws_up, ws_down, w_ple_gate, w_ple, ln2_g, ln2_b):
    weights = (w_in, g_q_lat, g_kv_lat, w_uq, w_uk, w_uv, g_q_gqa, g_k_gqa, g_o_mla, g_o_gqa, w_o,
               ln1_g, ln1_b, w_router, router_bias, w_gate, w_up, w_down, ws_gate, ws_up, ws_down,
               w_ple_gate, w_ple, ln2_g, ln2_b)
    b1, s1, _ = x_prompt.shape
    b2, s2, _ = x_sample.shape
    t1, t2 = b1 * s1, b2 * s2
    t = t1 + t2
    x = jnp.concatenate([x_prompt.reshape(t1, D_MODEL), x_sample.reshape(t2, D_MODEL)], axis=0)
    p = jnp.concatenate([p_prompt.reshape(DEPTH, t1, PLE_DIM), p_sample.reshape(DEPTH, t2, PLE_DIM)], axis=1)

    tm = _tile(min(s1, s2), 512)
    tm_final = _tile(min(s1, s2), 256)
    bm =_tile(t * TOP_K // N_EXPERTS, 512)
    n_blocks = t * TOP_K // bm + N_EXPERTS
    tables = _rope_tables(max(s1, s2))

    def pos_block(i):
        return jnp.where(i < t1 // tm, i % (s1 // tm), (i - t1 // tm) % (s2 // tm))

    for layer in range(DEPTH):
        wl = _layer_weights(layer, *weights)
        q_mla, k_mla, v_mla, q_gqa, k_gqa, v_gqa = _proj(x, wl, tables, pos_block, tm)
        groups = ((0, b1, s1), (t1, b2, s2))
        o_mla = jnp.concatenate([
            _attention(q_mla, k_mla, v_mla, row0=r0, n_seq=nb, seq=s, heads=MLA_HEADS, kv_group=1,
                       dq=MLA_QK_PAD, dv=MLA_V, tq=_tile(s, 512), tk=_tile(s, 2048), name="attn_mla")
            for r0, nb, s in groups], axis=0)
        o_gqa = jnp.concatenate([
            _attention(q_gqa, k_gqa, v_gqa, row0=r0, n_seq=nb, seq=s, heads=GQA_HEADS, kv_group=GQA_GROUP,
                       dq=GQA_HD, dv=GQA_HD, tq=_tile(s, 512), tk=_tile(s, 2048), name="attn_gqa")
            for r0, nb, s in groups], axis=0)
        x1, x1p = _outproj(x, o_mla, o_gqa, wl, tm)

        mask, gate, counts = _route(x1, wl['w_rt'], wl['bias'], tm)
        pad_start, fill_start, fill_n, blk_e, n_used = _slot_plan(counts[:, 0], bm, n_blocks)
        slot, w_k = _slots(mask, gate, pad_start.astype(F32).reshape(-1, 1), tm)
        slot_tok = slot.T
        y_in = _dispatch(x1p, slot_tok.reshape(t // tm, 1, tm * TOP_K), fill_start, fill_n, n_blocks * bm, tm)
        y_out = _experts(y_in, blk_e, n_used, w_gate, w_up, w_down, layer, bm)
        final = functools.partial(_final, slot_tok.reshape(t // tm_final, 1, tm_final * TOP_K), y_out, w_k.T, x1,
                                  p[layer], wl, tm_final)
        if layer < DEPTH - 1:
            x = final(0, t // tm_final)
        else:
            y_prompt = final(0, t1 // tm_final)
            y_sample = final(t1 // tm_final, t2 // tm_final)

    return y_prompt.reshape(b1, s1, D_MODEL), y_sample.reshape(b2, s2, D_MODEL)
```

```python
import functools

import jax
import jax.numpy as jnp
from jax import lax
from jax.experimental import pallas as pl
from jax.experimental.pallas import tpu as pltpu

F32 = jnp.float32
BF16 = jnp.bfloat16
U32 = jnp.uint32
I32 = jnp.int32

D_MODEL = 2048
DEPTH = 2
GRID_W = 64
ROPE_THETA = 10000.0
MLA_HEADS = 8
MLA_NOPE = 128
MLA_ROPE = 64
MLA_V = 128
MLA_QK_PAD = 256
Q_LORA = 512
KV_LORA = 512
MLA_WIDTH = MLA_HEADS * MLA_V
MLA_SCALE = (MLA_NOPE + MLA_ROPE) ** -0.5
GQA_HD = 128
GQA_HEADS = 8
GQA_KV_HEADS = 2
GQA_GROUP = GQA_HEADS // GQA_KV_HEADS
GQA_WIDTH = GQA_HEADS * GQA_HD
GQA_SCALE = GQA_HD ** -0.5
N_EXPERTS = 64
TOP_K = 8
N_GROUPS = 8
TOPK_GROUPS = 4
EXPERTS_PER_GROUP = N_EXPERTS // N_GROUPS
D_EXPERT = 512
ROUTED_SCALE = 2.5
PLE_DIM = 256
ALPHA = (2 * DEPTH) ** 0.25
RMS_EPS = 1e-6
LN_EPS = 1e-5
V_EXT = 256
LOG2E = 1.4426950408889634
TICKS_PER_TRIP = 2
HALF = D_MODEL // 2

LANES = 128
ROW_TILE = 8
VMEM_LIMIT = 56 * 1024 * 1024

C_CQ = 0
C_CKV = C_CQ + Q_LORA
C_KR = C_CKV + KV_LORA
C_QG = C_KR + LANES
C_KG = C_QG + GQA_WIDTH
C_VG = C_KG + GQA_KV_HEADS * GQA_HD
IN_COLS_PAD = C_VG + GQA_KV_HEADS * GQA_HD


def _tile(n, pref):
    t = min(n, pref)
    while n % t:
        t //= 2
    return t


def _const_spec(shape):
    nd = len(shape)
    return pl.BlockSpec(shape, lambda *_: (0,) * nd, pipeline_mode=pl.Buffered(1))


def _params(sem):
    return pltpu.CompilerParams(dimension_semantics=sem, vmem_limit_bytes=VMEM_LIMIT)


def _pack_pair(lo, hi):
    ulo = lax.bitcast_convert_type(lo.astype(BF16).astype(F32), U32)
    uhi = lax.bitcast_convert_type(hi.astype(BF16).astype(F32), U32)
    return (ulo >> 16) | uhi


def _unpack_pair(u):
    lo = lax.bitcast_convert_type(u << 16, F32)
    hi = lax.bitcast_convert_type(u & jnp.uint32(0xFFFF0000), F32)
    return lo, hi


def _rot_half_partner(x):
    lane = lax.broadcasted_iota(I32, x.shape, x.ndim - 1)
    n = x.shape[-1]
    up = pltpu.roll(x, n - 32, x.ndim - 1)
    down = pltpu.roll(x, 32, x.ndim - 1)
    return jnp.where((lane & 32) == 0, up, down)


def _rope(x, cos, sin_signed):
    return x * cos + _rot_half_partner(x) * sin_signed


def _rms(x, g):
    return x * lax.rsqrt(jnp.mean(x * x, axis=-1, keepdims=True) + RMS_EPS) * g


def _layer_norm(x, g, b):
    mu = jnp.mean(x, axis=-1, keepdims=True)
    xc = x - mu
    var = jnp.mean(xc * xc, axis=-1, keepdims=True)
    return xc * lax.rsqrt(var + LN_EPS) * g + b


def _dot(a, b):
    return jnp.dot(a, b, preferred_element_type=F32)


def _proj_kernel(x_ref, w_in_ref, w_uq_ref, w_uk_ref, w_uv_ref, g_q_lat_ref, g_kv_lat_ref,
                 g_q_gqa_ref, g_k_gqa_ref, cos_m_ref, sin_m_ref, cos_a_ref, sin_a_ref,
                 q_mla_ref, k_mla_ref, v_mla_ref, q_gqa_ref, k_gqa_ref, v_gqa_ref):
    xb = x_ref[...].astype(BF16)
    cos_m, sin_m = cos_m_ref[...], sin_m_ref[...]
    cos_a, sin_a = cos_a_ref[...], sin_a_ref[...]

    lat = _dot(xb, w_in_ref[:, C_CQ:C_QG])

    cq = _rms(lat[:, C_CQ:C_CQ + Q_LORA], g_q_lat_ref[...]).astype(BF16)
    q_all = _dot(cq, w_uq_ref[...])
    for h in range(MLA_HEADS):
        c0, c1 = h * MLA_QK_PAD, h * MLA_QK_PAD + MLA_NOPE
        q_rope = _rope(q_all[:, c1:c0 + MLA_QK_PAD], cos_m, sin_m)
        q_mla_ref[:, c0:c1] = (q_all[:, c0:c1] * (MLA_SCALE * LOG2E)).astype(BF16)
        q_mla_ref[:, c1:c0 + MLA_QK_PAD] = (q_rope * (MLA_SCALE * LOG2E)).astype(BF16)

    ckv = _rms(lat[:, C_CKV:C_CKV + KV_LORA], g_kv_lat_ref[...]).astype(BF16)
    k_rope = _rope(lat[:, C_KR:C_KR + LANES], cos_m, sin_m).astype(BF16)
    k_all = _dot(ckv, w_uk_ref[...]).astype(BF16)
    v_all = _dot(ckv, w_uv_ref[...]).astype(BF16)
    ones = jnp.ones((xb.shape[0], V_EXT - MLA_V), BF16)
    for h in range(MLA_HEADS):
        c0 = h * MLA_QK_PAD
        k_mla_ref[:, c0:c0 + MLA_NOPE] = k_all[:, h * MLA_NOPE:(h + 1) * MLA_NOPE]
        k_mla_ref[:, c0 + MLA_NOPE:c0 + MLA_QK_PAD] = k_rope
        v_mla_ref[:, h * V_EXT:h * V_EXT + MLA_V] = v_all[:, h * MLA_V:(h + 1) * MLA_V]
        v_mla_ref[:, h * V_EXT + MLA_V:(h + 1) * V_EXT] = ones

    g_q, g_k = g_q_gqa_ref[...], g_k_gqa_ref[...]
    qg_all = _dot(xb, w_in_ref[:, C_QG:C_KG])
    for h in range(GQA_HEADS):
        qh = _rope(_rms(qg_all[:, h * GQA_HD:(h + 1) * GQA_HD], g_q), cos_a, sin_a) * (GQA_SCALE * LOG2E)
        q_gqa_ref[:, h * GQA_HD:(h + 1) * GQA_HD] = qh.astype(BF16)
    kv_all = _dot(xb, w_in_ref[:, C_KG:IN_COLS_PAD])
    for h in range(GQA_KV_HEADS):
        kh = kv_all[:, h * GQA_HD:(h + 1) * GQA_HD]
        k_gqa_ref[:, h * GQA_HD:(h + 1) * GQA_HD] = _rope(_rms(kh, g_k), cos_a, sin_a).astype(BF16)
        vh = kv_all[:, C_VG - C_KG + h * GQA_HD:C_VG - C_KG + (h + 1) * GQA_HD]
        v_gqa_ref[:, h * V_EXT:h * V_EXT + GQA_HD] = vh.astype(BF16)
        v_gqa_ref[:, h * V_EXT + GQA_HD:(h + 1) * V_EXT] = ones


def _proj(x, wl, tables, pos_block, tm):
    t = x.shape[0]
    row = lambda w: pl.BlockSpec((tm, w), lambda i: (i, 0))
    tab = pl.BlockSpec((tm, LANES), lambda i: (pos_block(i), 0))
    widths = (MLA_HEADS * MLA_QK_PAD, MLA_HEADS * MLA_QK_PAD, MLA_HEADS * V_EXT,
              GQA_WIDTH, GQA_KV_HEADS * GQA_HD, GQA_KV_HEADS * V_EXT)
    consts = (wl['w_in_p'], wl['w_uq_p'], wl['w_uk'], wl['w_uv'],
              wl['g_q_lat'], wl['g_kv_lat'], wl['g_q_gqa'], wl['g_k_gqa'])
    return pl.pallas_call(
        _proj_kernel,
        grid=(t // tm,),
        in_specs=[row(D_MODEL)] + [_const_spec(c.shape) for c in consts] + [tab] * 4,
        out_specs=[row(w) for w in widths],
        out_shape=[jax.ShapeDtypeStruct((t, w), BF16) for w in widths],
        compiler_params=_params(("arbitrary",)),
        name="proj",
    )(x, *consts, *tables)


def _attn_kernel(q_ref, k_ref, v_ref, o_ref, s_ref, p_ref, a_ref, m_ref, acc_ref, *, tk, dv):
    q = q_ref[...]
    nk = k_ref.shape[0] // tk

    def scores(c, slot):
        off = pl.multiple_of(c * tk, tk)
        s_ref[slot] = lax.dot_general(q, k_ref[pl.ds(off, tk), :], (((1,), (1,)), ((), ())),
                                      preferred_element_type=F32)

    def softmax(slot):
        s = s_ref[slot]
        m = m_ref[...]
        m_new = jnp.maximum(m, jnp.max(s, axis=1, keepdims=True))
        a_ref[slot] = jnp.exp2(m - m_new)
        p_ref[slot] = jnp.exp2(s - m_new).astype(BF16)
        m_ref[...] = m_new

    def values(c, slot):
        off = pl.multiple_of(c * tk, tk)
        acc_ref[...] = a_ref[slot] * acc_ref[...] + _dot(p_ref[slot], v_ref[pl.ds(off, tk), :])

    def tick(t, parity, first=False, last=False):
        if not last:
            scores(t + 1, 1 - parity)
        if not first:
            values(t - 1, 1 - parity)
        softmax(parity)

    m_ref[...] = jnp.full(m_ref.shape, -jnp.inf, F32)
    acc_ref[...] = jnp.zeros(acc_ref.shape, F32)
    scores(0, 0)
    if nk <= 4:
        for t in range(nk):
            tick(t, t % 2, first=t == 0, last=t == nk - 1)
    else:
        assert nk % 2 == 0
        tick(0, 0, first=True)
        n_loop = (nk - 2) // TICKS_PER_TRIP

        def trip(i, c):
            t = TICKS_PER_TRIP * i + 1
            for d in range(TICKS_PER_TRIP):
                tick(t + d, (1 + d) % 2)
            return c
        lax.fori_loop(0, n_loop, trip, 0)
        for t in range(1 + n_loop * TICKS_PER_TRIP, nk - 1):
            tick(t, t % 2)
        tick(nk - 1, 1, last=True)
    values(nk - 1, (nk - 1) % 2)
    acc = acc_ref[...]
    o_ref[...] = (acc[:, :dv] / acc[:, dv:]).astype(o_ref.dtype)


def _attention(q, k, v, *, row0, n_seq, seq, heads, kv_group, dq, dv, tq, tk, name):
    assert row0 % seq == 0 and seq % tq == 0 and seq % tk == 0 and V_EXT == 2 * dv
    nq = seq // tq
    q0, s0 = row0 // tq, row0 // seq
    return pl.pallas_call(
        functools.partial(_attn_kernel, tk=tk, dv=dv),
        grid=(n_seq, heads, nq),
        in_specs=[pl.BlockSpec((tq, dq), lambda b, h, i: (q0 + b * nq + i, h)),
                  pl.BlockSpec((seq, dq), lambda b, h, i: (s0 + b, h // kv_group)),
                  pl.BlockSpec((seq, V_EXT), lambda b, h, i: (s0 + b, h // kv_group))],
        out_specs=pl.BlockSpec((tq, dv), lambda b, h, i: (b * nq + i, h)),
        out_shape=jax.ShapeDtypeStruct((n_seq * seq, heads * dv), BF16),
        scratch_shapes=[pltpu.VMEM((2, tq, tk), F32), pltpu.VMEM((2, tq, tk), BF16),
                        pltpu.VMEM((2, tq, 1), F32), pltpu.VMEM((tq, 1), F32), pltpu.VMEM((tq, V_EXT), F32)],
        compiler_params=_params(("arbitrary", "arbitrary", "arbitrary")),
        name=name,
    )(q, k, v)


def _outproj_kernel(x_ref, om_ref, og_ref, g_om_ref, g_og_ref, w_o_ref, ln_g_ref, ln_b_ref,
                    x1_ref, x1p_ref):
    om = _rms(om_ref[...].astype(F32), g_om_ref[...]).astype(BF16)
    og = _rms(og_ref[...].astype(F32), g_og_ref[...]).astype(BF16)
    mix = _dot(om, w_o_ref[:MLA_WIDTH, :]) + _dot(og, w_o_ref[MLA_WIDTH:, :])
    x1 = _layer_norm(ALPHA * x_ref[...] + mix, ln_g_ref[...], ln_b_ref[...])
    x1_ref[...] = x1
    x1p_ref[...] = _pack_pair(x1[:, :HALF], x1[:, HALF:])


def _outproj(x, o_mla, o_gqa, wl, tm):
    t = x.shape[0]
    row = lambda w: pl.BlockSpec((tm, w), lambda i: (i, 0))
    consts = (wl['g_o_mla'], wl['g_o_gqa'], wl['w_o'], wl['ln1_g'], wl['ln1_b'])
    return pl.pallas_call(
        _outproj_kernel,
        grid=(t // tm,),
        in_specs=[row(D_MODEL), row(MLA_WIDTH), row(GQA_WIDTH)] + [_const_spec(c.shape) for c in consts],
        out_specs=[row(D_MODEL), row(HALF)],
        out_shape=[jax.ShapeDtypeStruct((t, D_MODEL), F32), jax.ShapeDtypeStruct((t, HALF), U32)],
        compiler_params=_params(("arbitrary",)),
        name="outproj",
    )(x, o_mla, o_gqa, *consts)


def _beats(a, ia, b, ib):
    return (a > b) | ((a == b) & (ia < ib))


def _route_kernel(x_ref, w_rt_ref, bias_ref, mask_ref, gate_ref, counts_ref):
    logits = lax.dot_general(w_rt_ref[...], x_ref[...], (((1,), (1,)), ((), ())),
                             precision=lax.Precision.HIGHEST, preferred_element_type=F32)
    scores = jax.nn.sigmoid(logits)
    sel = scores + bias_ref[...]
    tm = sel.shape[1]
    neg = jnp.float32(-jnp.inf)

    sub = lax.broadcasted_iota(I32, (EXPERTS_PER_GROUP, tm), 0)
    grp_rows = []
    for g in range(N_GROUPS):
        blk = sel[g * EXPERTS_PER_GROUP:(g + 1) * EXPERTS_PER_GROUP, :]
        m1 = jnp.max(blk, axis=0, keepdims=True)
        first = jnp.min(jnp.where(blk == m1, sub, EXPERTS_PER_GROUP), axis=0, keepdims=True)
        m2 = jnp.max(jnp.where(sub == first, neg, blk), axis=0, keepdims=True)
        grp_rows.append(m1 + m2)
    grp = jnp.concatenate(grp_rows, axis=0)

    gidx = lax.broadcasted_iota(I32, (N_GROUPS, tm), 0)
    grank = jnp.zeros((N_GROUPS, tm), I32)
    for g in range(N_GROUPS):
        grank += _beats(grp[g:g + 1, :], g, grp, gidx).astype(I32)
    gkeep = grank < TOPK_GROUPS
    keep = jnp.concatenate(
        [jnp.broadcast_to(gkeep[g:g + 1, :], (EXPERTS_PER_GROUP, tm)) for g in range(N_GROUPS)], axis=0)
    sel = jnp.where(keep, sel, neg)

    eidx = lax.broadcasted_iota(I32, (N_EXPERTS, tm), 0)
    erank = jnp.zeros((N_EXPERTS, tm), I32)
    for e in range(N_EXPERTS):
        erank += _beats(sel[e:e + 1, :], e, sel, eidx).astype(I32)
    chosen = (erank < TOP_K) & keep

    picked = jnp.where(chosen, scores, 0.0)
    gate = picked / jnp.sum(picked, axis=0, keepdims=True) * ROUTED_SCALE
    mask = chosen.astype(F32)
    mask_ref[...] = mask
    gate_ref[...] = gate

    @pl.when(pl.program_id(0) == 0)
    def _():
        counts_ref[...] = jnp.zeros_like(counts_ref)
    counts_ref[...] += jnp.broadcast_to(jnp.sum(mask, axis=1, keepdims=True), counts_ref.shape)


def _route(x1, w_rt, bias, tm):
    t = x1.shape[0]
    col = pl.BlockSpec((N_EXPERTS, tm), lambda i: (0, i))
    return pl.pallas_call(
        _route_kernel,
        grid=(t // tm,),
        in_specs=[pl.BlockSpec((tm, D_MODEL), lambda i: (i, 0)),
                  _const_spec(w_rt.shape), _const_spec(bias.shape)],
        out_specs=[col, col, pl.BlockSpec((N_EXPERTS, LANES), lambda i: (0, 0))],
        out_shape=[jax.ShapeDtypeStruct((N_EXPERTS, t), F32), jax.ShapeDtypeStruct((N_EXPERTS, t), F32),
                   jax.ShapeDtypeStruct((N_EXPERTS, LANES), F32)],
        compiler_params=_params(("arbitrary",)),
        name="route",
    )(x1, w_rt, bias)


def _slots_kernel(mask_ref, gate_ref, start_ref, slot_ref, w_ref, seen_ref):
    @pl.when(pl.program_id(0) == 0)
    def _():
        seen_ref[...] = jnp.zeros_like(seen_ref)

    mask = mask_ref[...]
    tm = mask.shape[1]
    mb = mask.astype(BF16)
    before = (lax.broadcasted_iota(I32, (tm, tm), 0) < lax.broadcasted_iota(I32, (tm, tm), 1)).astype(BF16)
    pos = _dot(mb, before)
    lower = (lax.broadcasted_iota(I32, (N_EXPERTS, N_EXPERTS), 1)
             < lax.broadcasted_iota(I32, (N_EXPERTS, N_EXPERTS), 0)).astype(BF16)
    rank = _dot(lower, mb)
    seen = seen_ref[:, 0:1]
    slot = start_ref[...] + seen + pos
    on = mask > 0.5
    gate = gate_ref[...]
    for k in range(TOP_K):
        pick = on & (rank == float(k))
        slot_ref[k:k + 1, :] = jnp.sum(jnp.where(pick, slot, 0.0), axis=0, keepdims=True).astype(I32)
        w_ref[k:k + 1, :] = jnp.sum(jnp.where(pick, gate, 0.0), axis=0, keepdims=True)
    seen_ref[...] += jnp.broadcast_to(jnp.sum(mask, axis=1, keepdims=True), seen_ref.shape)


def _slots(mask, gate, start, tm):
    t = mask.shape[1]
    col = pl.BlockSpec((N_EXPERTS, tm), lambda i: (0, i))
    out = pl.BlockSpec((TOP_K, tm), lambda i: (0, i))
    return pl.pallas_call(
        _slots_kernel,
        grid=(t // tm,),
        in_specs=[col, col, _const_spec(start.shape)],
        out_specs=[out, out],
        out_shape=[jax.ShapeDtypeStruct((TOP_K, t), I32), jax.ShapeDtypeStruct((TOP_K, t), F32)],
        scratch_shapes=[pltpu.VMEM((N_EXPERTS, LANES), F32)],
        compiler_params=_params(("arbitrary",)),
        name="slots",
    )(mask, gate, start)


def _dispatch_kernel(fill_start_ref, fill_n_ref, slot_ref, x_ref, y_hbm, zero_ref, sem, fill_sem, *, tm):
    def issue(t, c):
        for k in range(TOP_K):
            slot = slot_ref[0, 0, t * TOP_K + k]
            pltpu.make_async_copy(x_ref.at[pl.ds(t, 1)], y_hbm.at[pl.ds(slot, 1)], sem).start(priority=k % 2)
        return c
    lax.fori_loop(0, tm, issue, 0)

    @pl.when(pl.program_id(0) == 0)
    def _():
        zero_ref[...] = jnp.zeros_like(zero_ref)

        def fill_copy(r):
            return pltpu.make_async_copy(zero_ref, y_hbm.at[pl.ds(r, 1)], fill_sem)

        def per_expert(e, c):
            lax.fori_loop(0, fill_n_ref[e], lambda j, c2: (fill_copy(fill_start_ref[e] + j).start(), c2)[1], 0)
            lax.fori_loop(0, fill_n_ref[e], lambda j, c2: (fill_copy(fill_start_ref[e] + j).wait(), c2)[1], 0)
            return c
        lax.fori_loop(0, N_EXPERTS, per_expert, 0)

    rows = y_hbm.at[pl.ds(0, TOP_K * tm)]
    pltpu.make_async_copy(rows, rows, sem).wait()


def _dispatch(x1p, slot, fill_start, fill_n, n_slots, tm):
    t = x1p.shape[0]
    assert TOP_K * tm <= n_slots
    return pl.pallas_call(
        functools.partial(_dispatch_kernel, tm=tm),
        grid_spec=pltpu.PrefetchScalarGridSpec(
            num_scalar_prefetch=2,
            grid=(t // tm,),
            in_specs=[pl.BlockSpec((1, 1, tm * TOP_K), lambda i, *_: (i, 0, 0), memory_space=pltpu.SMEM),
                      pl.BlockSpec((tm, HALF), lambda i, *_: (i, 0))],
            out_specs=pl.BlockSpec(memory_space=pl.ANY),
            scratch_shapes=[pltpu.VMEM((1, HALF), U32), pltpu.SemaphoreType.DMA, pltpu.SemaphoreType.DMA]),
        out_shape=jax.ShapeDtypeStruct((n_slots, HALF), U32),
        compiler_params=pltpu.CompilerParams(dimension_semantics=("arbitrary",), has_side_effects=True),
        name="dispatch",
    )(fill_start, fill_n, slot, x1p)


def _experts_kernel(blk_e_ref, n_used_ref, y_ref, wg_ref, wu_ref, wd_ref, o_ref, wg_s, wu_s, wd_s):
    b = pl.program_id(0)

    @pl.when(b < n_used_ref[0])
    def _():
        @pl.when((b == 0) | (blk_e_ref[b] != blk_e_ref[jnp.maximum(b - 1, 0)]))
        def _():
            wg_s[...] = wg_ref[0, 0].astype(BF16)
            wu_s[...] = wu_ref[0, 0].astype(BF16)
            wd_s[...] = wd_ref[0, 0].astype(BF16)

        lo, hi = _unpack_pair(y_ref[...])
        x = jnp.concatenate([lo.astype(BF16), hi.astype(BF16)], axis=1)
        hid = (jax.nn.silu(_dot(x, wg_s[...])) * _dot(x, wu_s[...])).astype(BF16)
        out = _dot(hid, wd_s[...])
        packed = _pack_pair(out[:, :HALF], out[:, HALF:])
        for j in range(ROW_TILE):
            o_ref[pl.ds(j, packed.shape[0], stride=ROW_TILE), :] = packed[:, j * LANES:(j + 1) * LANES]


def _experts(y_in, blk_e, n_used, w_gate, w_up, w_down, layer, bm):
    n_slots = y_in.shape[0]
    blk = lambda b, be, nu: (jnp.minimum(b, nu[0] - 1), 0)
    wsel = lambda b, be, nu: (layer, be[b], 0, 0)
    return pl.pallas_call(
        _experts_kernel,
        grid_spec=pltpu.PrefetchScalarGridSpec(
            num_scalar_prefetch=2,
            grid=(n_slots // bm,),
            in_specs=[pl.BlockSpec((bm, HALF), blk),
                      pl.BlockSpec((1, 1, D_MODEL, D_EXPERT), wsel),
                      pl.BlockSpec((1, 1, D_MODEL, D_EXPERT), wsel),
                      pl.BlockSpec((1, 1, D_EXPERT, D_MODEL), wsel)],
            out_specs=pl.BlockSpec((bm * ROW_TILE, LANES), blk),
            scratch_shapes=[pltpu.VMEM((D_MODEL, D_EXPERT), BF16), pltpu.VMEM((D_MODEL, D_EXPERT), BF16),
                            pltpu.VMEM((D_EXPERT, D_MODEL), BF16)]),
        out_shape=jax.ShapeDtypeStruct((n_slots * ROW_TILE, LANES), U32),
        compiler_params=_params(("arbitrary",)),
        name="experts",
    )(blk_e, n_used, y_in, w_gate, w_up, w_down)


def _final_kernel(slot_ref, slot_next_ref, y_hbm, w_ref, x1_ref, p_ref, ws_gate_ref, ws_up_ref, ws_down_ref,
                  w_pg_ref, w_ple_ref, ln_g_ref, ln_b_ref, o_ref, rows_ref, sem, *, tm):
    i = pl.program_id(0)
    cur = i % 2

    def gather(ids_ref, buf):
        def issue(t, c):
            dst_row = pl.multiple_of(t * ROW_TILE, ROW_TILE)
            for k in range(TOP_K):
                src_row = pl.multiple_of(ids_ref[0, 0, t * TOP_K + k] * ROW_TILE, ROW_TILE)
                pltpu.make_async_copy(y_hbm.at[pl.ds(src_row, ROW_TILE)],
                                      rows_ref.at[buf, k, pl.ds(dst_row, ROW_TILE)], sem.at[buf]).start()
            return c
        lax.fori_loop(0, tm, issue, 0)

    @pl.when(i == 0)
    def _():
        gather(slot_ref, 0)

    has_next = i + 1 < pl.num_programs(0)
    for buf in range(2):
        @pl.when(has_next & (cur != buf))
        def _():
            gather(slot_next_ref, buf)

    x1 = x1_ref[...]
    xb = x1.astype(BF16)
    hid = (jax.nn.silu(_dot(xb, ws_gate_ref[...])) * _dot(xb, ws_up_ref[...])).astype(BF16)
    shared = _dot(hid, ws_down_ref[...])
    ple = jax.nn.sigmoid(_dot(xb, w_pg_ref[...])) * _dot(p_ref[...].astype(BF16), w_ple_ref[...])
    base = ALPHA * x1 + shared + ple

    pltpu.make_async_copy(rows_ref.at[cur], rows_ref.at[cur], sem.at[cur]).wait()

    w = w_ref[...]
    wk = [jnp.broadcast_to(w[:, k:k + 1], (tm, LANES)) for k in range(TOP_K)]
    r_lo, r_hi = [], []
    for j in range(ROW_TILE):
        a_lo = jnp.zeros((tm, LANES), F32)
        a_hi = jnp.zeros((tm, LANES), F32)
        for k in range(TOP_K):
            lo, hi = _unpack_pair(rows_ref[cur, k, pl.ds(j, tm, stride=ROW_TILE), :])
            a_lo += wk[k] * lo
            a_hi += wk[k] * hi
        r_lo.append(a_lo)
        r_hi.append(a_hi)
    y = base + jnp.concatenate(r_lo + r_hi, axis=1)
    o_ref[...] = _layer_norm(y, ln_g_ref[...], ln_b_ref[...])


def _final(slot, y_out, w_tok, x1, p, wl, tm, tile0, n):
    row = lambda w: pl.BlockSpec((tm, w), lambda i: (tile0 + i, 0))
    ids = lambda index: pl.BlockSpec((1, 1, tm * TOP_K), index, memory_space=pltpu.SMEM)
    consts = (wl['ws_gate'], wl['ws_up'], wl['ws_down'], wl['w_ple_gate'], wl['w_ple'], wl['ln2_g'], wl['ln2_b'])
    return pl.pallas_call(
        functools.partial(_final_kernel, tm=tm),
        grid=(n,),
        in_specs=[ids(lambda i: (tile0 + i, 0, 0)), ids(lambda i: (tile0 + jnp.minimum(i + 1, n - 1), 0, 0)),
                  pl.BlockSpec(memory_space=pl.ANY),
                  row(TOP_K), row(D_MODEL), row(PLE_DIM)] + [_const_spec(c.shape) for c in consts],
        out_specs=pl.BlockSpec((tm, D_MODEL), lambda i: (i, 0)),
        out_shape=jax.ShapeDtypeStruct((n * tm, D_MODEL), F32),
        scratch_shapes=[pltpu.VMEM((2, TOP_K, tm * ROW_TILE, LANES), U32), pltpu.SemaphoreType.DMA((2,))],
        compiler_params=_params(("arbitrary",)),
        name="final",
    )(slot, slot, y_out, w_tok, x1, p, *consts)


def _rope_tables(n):
    t = jnp.arange(n, dtype=F32)
    inv = ROPE_THETA ** (-jnp.arange(0, MLA_ROPE, 2, dtype=F32) / MLA_ROPE)
    a_seq = t[:, None] * inv[None, :]
    a_row = jnp.floor(t / GRID_W)[:, None] * inv[None, :]
    a_col = (t - jnp.floor(t / GRID_W) * GRID_W)[:, None] * inv[None, :]
    z = jnp.zeros((n, LANES // 2), F32)
    cos_m = jnp.concatenate([jnp.cos(a_seq), jnp.cos(a_seq), z], axis=1)
    sin_m = jnp.concatenate([-jnp.sin(a_seq), jnp.sin(a_seq), z], axis=1)
    cos_a = jnp.concatenate([jnp.cos(a_row), jnp.cos(a_row), jnp.cos(a_col), jnp.cos(a_col)], axis=1)
    sin_a = jnp.concatenate([-jnp.sin(a_row), jnp.sin(a_row), -jnp.sin(a_col), jnp.sin(a_col)], axis=1)
    return cos_m, sin_m, cos_a, sin_a


def _layer_weights(i, w_in, g_q_lat, g_kv_lat, w_uq, w_uk, w_uv, g_q_gqa, g_k_gqa, g_o_mla, g_o_gqa,
                   w_o, ln1_g, ln1_b, w_router, router_bias, w_gate, w_up, w_down, ws_gate, ws_up,
                   ws_down, w_ple_gate, w_ple, ln2_g, ln2_b):
    wi = w_in[i]
    kr0 = Q_LORA + KV_LORA
    w_in_p = jnp.concatenate(
        [wi[:, :kr0 + MLA_ROPE], jnp.zeros((D_MODEL, LANES - MLA_ROPE), F32), wi[:, kr0 + MLA_ROPE:]], axis=1)
    uq = w_uq[i].reshape(Q_LORA, MLA_HEADS, MLA_NOPE + MLA_ROPE)
    uq = jnp.pad(uq, ((0, 0), (0, 0), (0, MLA_QK_PAD - MLA_NOPE - MLA_ROPE)))
    vec = lambda a: a[i].reshape(1, -1).astype(F32)
    return dict(
        w_in_p=w_in_p.astype(BF16), w_uq_p=uq.reshape(Q_LORA, MLA_HEADS * MLA_QK_PAD).astype(BF16),
        w_uk=w_uk[i].astype(BF16), w_uv=w_uv[i].astype(BF16),
        g_q_lat=vec(g_q_lat), g_kv_lat=vec(g_kv_lat), g_q_gqa=vec(g_q_gqa), g_k_gqa=vec(g_k_gqa),
        g_o_mla=vec(g_o_mla), g_o_gqa=vec(g_o_gqa), w_o=w_o[i].astype(BF16),
        ln1_g=vec(ln1_g), ln1_b=vec(ln1_b),
        w_rt=w_router[i].T.astype(F32), bias=router_bias[i].reshape(-1, 1).astype(F32),
        ws_gate=ws_gate[i].astype(BF16), ws_up=ws_up[i].astype(BF16), ws_down=ws_down[i].astype(BF16),
        w_ple_gate=w_ple_gate[i].astype(BF16), w_ple=w_ple[i].astype(BF16),
        ln2_g=vec(ln2_g), ln2_b=vec(ln2_b))


def _slot_plan(counts, bm, n_blocks):
    counts = counts.astype(I32)
    padded = (counts + bm - 1) // bm * bm
    pad_end = jnp.cumsum(padded)
    pad_start = pad_end - padded
    first_row = jnp.arange(n_blocks, dtype=I32) * bm
    blk_e = jnp.minimum(jnp.sum(pad_end[None, :] <= first_row[:, None], axis=1), N_EXPERTS - 1).astype(I32)
    n_used = jnp.maximum(pad_end[-1:] // bm, 1).astype(I32)
    return pad_start, pad_start + counts, padded - counts, blk_e, n_used


def kernel(x_prompt, x_sample, p_prompt, p_sample, w_in, g_q_lat, g_kv_lat, w_uq, w_uk, w_uv,
           g_q_gqa, g_k_gqa, g_o_mla, g_o_gqa, w_o, ln1_g, ln1_b, w_router, router_bias,
           w_gate, w_up, w_down, ws_gate, ws_up, ws_down, w_ple_gate, w_ple, ln2_g, ln2_b):
    weights = (w_in, g_q_lat, g_kv_lat, w_uq, w_uk, w_uv, g_q_gqa, g_k_gqa, g_o_mla, g_o_gqa, w_o,
               ln1_g, ln1_b, w_router, router_bias, w_gate, w_up, w_down, ws_gate, ws_up, ws_down,
               w_ple_gate, w_ple, ln2_g, ln2_b)
    b1, s1, _ = x_prompt.shape
    b2, s2, _ = x_sample.shape
    t1, t2 = b1 * s1, b2 * s2
    t = t1 + t2
    x = jnp.concatenate([x_prompt.reshape(t1, D_MODEL), x_sample.reshape(t2, D_MODEL)], axis=0)
    p = jnp.concatenate([p_prompt.reshape(DEPTH, t1, PLE_DIM), p_sample.reshape(DEPTH, t2, PLE_DIM)], axis=1)

    tm = _tile(min(s1, s2), 512)
    tm_final = _tile(min(s1, s2), 256)
    bm =_tile(t * TOP_K // N_EXPERTS, 512)
    n_blocks = t * TOP_K // bm + N_EXPERTS
    tables = _rope_tables(max(s1, s2))

    def pos_block(i):
        return jnp.where(i < t1 // tm, i % (s1 // tm), (i - t1 // tm) % (s2 // tm))

    for layer in range(DEPTH):
        wl = _layer_weights(layer, *weights)
        q_mla, k_mla, v_mla, q_gqa, k_gqa, v_gqa = _proj(x, wl, tables, pos_block, tm)
        groups = ((0, b1, s1), (t1, b2, s2))
        o_mla = jnp.concatenate([
            _attention(q_mla, k_mla, v_mla, row0=r0, n_seq=nb, seq=s, heads=MLA_HEADS, kv_group=1,
                       dq=MLA_QK_PAD, dv=MLA_V, tq=_tile(s, 512), tk=_tile(s, 2048), name="attn_mla")
            for r0, nb, s in groups], axis=0)
        o_gqa = jnp.concatenate([
            _attention(q_gqa, k_gqa, v_gqa, row0=r0, n_seq=nb, seq=s, heads=GQA_HEADS, kv_group=GQA_GROUP,
                       dq=GQA_HD, dv=GQA_HD, tq=_tile(s, 512), tk=_tile(s, 2048), name="attn_gqa")
            for r0, nb, s in groups], axis=0)
        x1, x1p = _outproj(x, o_mla, o_gqa, wl, tm)

        mask, gate, counts = _route(x1, wl['w_rt'], wl['bias'], tm)
        pad_start, fill_start, fill_n, blk_e, n_used = _slot_plan(counts[:, 0], bm, n_blocks)
        slot, w_k = _slots(mask, gate, pad_start.astype(F32).reshape(-1, 1), tm)
        slot_tok = slot.T
        y_in = _dispatch(x1p, slot_tok.reshape(t // tm, 1, tm * TOP_K), fill_start, fill_n, n_blocks * bm, tm)
        y_out = _experts(y_in, blk_e, n_used, w_gate, w_up, w_down, layer, bm)
        final = functools.partial(_final, slot_tok.reshape(t // tm_final, 1, tm_final * TOP_K), y_out, w_k.T, x1,
                                  p[layer], wl, tm_final)
        if layer < DEPTH - 1:
            x = final(0, t // tm_final)
        else:
            y_prompt = final(0, t1 // tm_final)
            y_sample = final(t1 // tm_final, t2 // tm_final)

    return y_prompt.reshape(b1, s1, D_MODEL), y_sample.reshape(b2, s2, D_MODEL)
```

```python
import functools

import jax
import jax.numpy as jnp
from jax import lax
from jax.experimental import pallas as pl
from jax.experimental.pallas import tpu as pltpu

F32 = jnp.float32
BF16 = jnp.bfloat16
U32 = jnp.uint32
I32 = jnp.int32

D_MODEL = 2048
DEPTH = 2
GRID_W = 64
ROPE_THETA = 10000.0
MLA_HEADS = 8
MLA_NOPE = 128
MLA_ROPE = 64
MLA_V = 128
MLA_QK_PAD = 256
Q_LORA = 512
KV_LORA = 512
MLA_WIDTH = MLA_HEADS * MLA_V
MLA_SCALE = (MLA_NOPE + MLA_ROPE) ** -0.5
GQA_HD = 128
GQA_HEADS = 8
GQA_KV_HEADS = 2
GQA_GROUP = GQA_HEADS // GQA_KV_HEADS
GQA_WIDTH = GQA_HEADS * GQA_HD
GQA_SCALE = GQA_HD ** -0.5
N_EXPERTS = 64
TOP_K = 8
N_GROUPS = 8
TOPK_GROUPS = 4
EXPERTS_PER_GROUP = N_EXPERTS // N_GROUPS
D_EXPERT = 512
ROUTED_SCALE = 2.5
PLE_DIM = 256
ALPHA = (2 * DEPTH) ** 0.25
RMS_EPS = 1e-6
LN_EPS = 1e-5
V_EXT = 256
LOG2E = 1.4426950408889634
MAX_STATIC_TICKS = 8
TICKS_PER_TRIP = 2
HALF = D_MODEL // 2

LANES = 128
ROW_TILE = 8
VMEM_LIMIT = 56 * 1024 * 1024

C_CQ = 0
C_CKV = C_CQ + Q_LORA
C_KR = C_CKV + KV_LORA
C_QG = C_KR + LANES
C_KG = C_QG + GQA_WIDTH
C_VG = C_KG + GQA_KV_HEADS * GQA_HD
IN_COLS_PAD = C_VG + GQA_KV_HEADS * GQA_HD


def _tile(n, pref):
    t = min(n, pref)
    while n % t:
        t //= 2
    return t


def _const_spec(shape):
    nd = len(shape)
    return pl.BlockSpec(shape, lambda *_: (0,) * nd, pipeline_mode=pl.Buffered(1))


def _params(sem):
    return pltpu.CompilerParams(dimension_semantics=sem, vmem_limit_bytes=VMEM_LIMIT)


def _pack_pair(lo, hi):
    ulo = lax.bitcast_convert_type(lo.astype(BF16).astype(F32), U32)
    uhi = lax.bitcast_convert_type(hi.astype(BF16).astype(F32), U32)
    return (ulo >> 16) | uhi


def _unpack_pair(u):
    lo = lax.bitcast_convert_type(u << 16, F32)
    hi = lax.bitcast_convert_type(u & jnp.uint32(0xFFFF0000), F32)
    return lo, hi


def _rot_half_partner(x):
    lane = lax.broadcasted_iota(I32, x.shape, x.ndim - 1)
    n = x.shape[-1]
    up = pltpu.roll(x, n - 32, x.ndim - 1)
    down = pltpu.roll(x, 32, x.ndim - 1)
    return jnp.where((lane & 32) == 0, up, down)


def _rope(x, cos, sin_signed):
    return x * cos + _rot_half_partner(x) * sin_signed


def _rms(x, g):
    return x * lax.rsqrt(jnp.mean(x * x, axis=-1, keepdims=True) + RMS_EPS) * g


def _layer_norm(x, g, b):
    mu = jnp.mean(x, axis=-1, keepdims=True)
    xc = x - mu
    var = jnp.mean(xc * xc, axis=-1, keepdims=True)
    return xc * lax.rsqrt(var + LN_EPS) * g + b


def _dot(a, b):
    return jnp.dot(a, b, preferred_element_type=F32)


def _proj_kernel(x_ref, w_in_ref, w_uq_ref, w_uk_ref, w_uv_ref, g_q_lat_ref, g_kv_lat_ref,
                 g_q_gqa_ref, g_k_gqa_ref, cos_m_ref, sin_m_ref, cos_a_ref, sin_a_ref,
                 q_mla_ref, k_mla_ref, v_mla_ref, q_gqa_ref, k_gqa_ref, v_gqa_ref):
    xb = x_ref[...].astype(BF16)
    cos_m, sin_m = cos_m_ref[...], sin_m_ref[...]
    cos_a, sin_a = cos_a_ref[...], sin_a_ref[...]

    lat = _dot(xb, w_in_ref[:, C_CQ:C_QG])

    cq = _rms(lat[:, C_CQ:C_CQ + Q_LORA], g_q_lat_ref[...]).astype(BF16)
    q_all = _dot(cq, w_uq_ref[...])
    for h in range(MLA_HEADS):
        c0, c1 = h * MLA_QK_PAD, h * MLA_QK_PAD + MLA_NOPE
        q_rope = _rope(q_all[:, c1:c0 + MLA_QK_PAD], cos_m, sin_m)
        q_mla_ref[:, c0:c1] = (q_all[:, c0:c1] * (MLA_SCALE * LOG2E)).astype(BF16)
        q_mla_ref[:, c1:c0 + MLA_QK_PAD] = (q_rope * (MLA_SCALE * LOG2E)).astype(BF16)

    ckv = _rms(lat[:, C_CKV:C_CKV + KV_LORA], g_kv_lat_ref[...]).astype(BF16)
    k_rope = _rope(lat[:, C_KR:C_KR + LANES], cos_m, sin_m).astype(BF16)
    k_all = _dot(ckv, w_uk_ref[...]).astype(BF16)
    v_all = _dot(ckv, w_uv_ref[...]).astype(BF16)
    ones = jnp.ones((xb.shape[0], V_EXT - MLA_V), BF16)
    for h in range(MLA_HEADS):
        c0 = h * MLA_QK_PAD
        k_mla_ref[:, c0:c0 + MLA_NOPE] = k_all[:, h * MLA_NOPE:(h + 1) * MLA_NOPE]
        k_mla_ref[:, c0 + MLA_NOPE:c0 + MLA_QK_PAD] = k_rope
        v_mla_ref[:, h * V_EXT:h * V_EXT + MLA_V] = v_all[:, h * MLA_V:(h + 1) * MLA_V]
        v_mla_ref[:, h * V_EXT + MLA_V:(h + 1) * V_EXT] = ones

    g_q, g_k = g_q_gqa_ref[...], g_k_gqa_ref[...]
    qg_all = _dot(xb, w_in_ref[:, C_QG:C_KG])
    for h in range(GQA_HEADS):
        qh = _rope(_rms(qg_all[:, h * GQA_HD:(h + 1) * GQA_HD], g_q), cos_a, sin_a) * (GQA_SCALE * LOG2E)
        q_gqa_ref[:, h * GQA_HD:(h + 1) * GQA_HD] = qh.astype(BF16)
    kv_all = _dot(xb, w_in_ref[:, C_KG:IN_COLS_PAD])
    for h in range(GQA_KV_HEADS):
        kh = kv_all[:, h * GQA_HD:(h + 1) * GQA_HD]
        k_gqa_ref[:, h * GQA_HD:(h + 1) * GQA_HD] = _rope(_rms(kh, g_k), cos_a, sin_a).astype(BF16)
        vh = kv_all[:, C_VG - C_KG + h * GQA_HD:C_VG - C_KG + (h + 1) * GQA_HD]
        v_gqa_ref[:, h * V_EXT:h * V_EXT + GQA_HD] = vh.astype(BF16)
        v_gqa_ref[:, h * V_EXT + GQA_HD:(h + 1) * V_EXT] = ones


def _proj(x, wl, tables, pos_block, tm):
    t = x.shape[0]
    row = lambda w: pl.BlockSpec((tm, w), lambda i: (i, 0))
    tab = pl.BlockSpec((tm, LANES), lambda i: (pos_block(i), 0))
    widths = (MLA_HEADS * MLA_QK_PAD, MLA_HEADS * MLA_QK_PAD, MLA_HEADS * V_EXT,
              GQA_WIDTH, GQA_KV_HEADS * GQA_HD, GQA_KV_HEADS * V_EXT)
    consts = (wl['w_in_p'], wl['w_uq_p'], wl['w_uk'], wl['w_uv'],
              wl['g_q_lat'], wl['g_kv_lat'], wl['g_q_gqa'], wl['g_k_gqa'])
    return pl.pallas_call(
        _proj_kernel,
        grid=(t // tm,),
        in_specs=[row(D_MODEL)] + [_const_spec(c.shape) for c in consts] + [tab] * 4,
        out_specs=[row(w) for w in widths],
        out_shape=[jax.ShapeDtypeStruct((t, w), BF16) for w in widths],
        compiler_params=_params(("arbitrary",)),
        name="proj",
    )(x, *consts, *tables)


def _attn_kernel(q_ref, k_ref, v_ref, o_ref, s_ref, p_ref, a_ref, m_ref, acc_ref, *, tq, tk, dv):
    nq = q_ref.shape[0] // tq
    nk = k_ref.shape[0] // tk
    n_items = nq * nk

    def split(i):
        return (0, i) if nq == 1 else (i // nk, i % nk)

    def scores(i, slot):
        qt, c = split(i)
        off = pl.multiple_of(c * tk, tk)
        s_ref[slot] = lax.dot_general(q_ref[pl.ds(qt * tq, tq), :], k_ref[pl.ds(off, tk), :],
                                      (((1,), (1,)), ((), ())), preferred_element_type=F32)

    def softmax(i, slot):
        qt, _ = split(i)
        s = s_ref[slot]
        m = m_ref[qt]
        m_new = jnp.maximum(m, jnp.max(s, axis=1, keepdims=True))
        a_ref[slot] = jnp.exp2(m - m_new)
        p_ref[slot] = jnp.exp2(s - m_new).astype(BF16)
        m_ref[qt] = m_new

    def values(i, slot):
        qt, c = split(i)
        off = pl.multiple_of(c * tk, tk)
        acc_ref[qt] = a_ref[slot] * acc_ref[qt] + _dot(p_ref[slot], v_ref[pl.ds(off, tk), :])

    def tick(i, parity, first=False, last=False):
        if not last:
            scores(i + 1, 1 - parity)
        if not first:
            values(i - 1, 1 - parity)
        softmax(i, parity)

    m_ref[...] = jnp.full(m_ref.shape, -jnp.inf, F32)
    acc_ref[...] = jnp.zeros(acc_ref.shape, F32)
    scores(0, 0)
    if n_items <= MAX_STATIC_TICKS:
        for i in range(n_items):
            tick(i, i % 2, first=i == 0, last=i == n_items - 1)
    else:
        assert nq == 1 and nk % 2 == 0
        tick(0, 0, first=True)
        n_loop = (nk - 2) // TICKS_PER_TRIP

        def trip(j, c):
            i = TICKS_PER_TRIP * j + 1
            for d in range(TICKS_PER_TRIP):
                tick(i + d, (1 + d) % 2)
            return c
        lax.fori_loop(0, n_loop, trip, 0)
        for i in range(1 + n_loop * TICKS_PER_TRIP, nk - 1):
            tick(i, i % 2)
        tick(nk - 1, 1, last=True)
    values(n_items - 1, (n_items - 1) % 2)
    for qt in range(nq):
        acc = acc_ref[qt]
        o_ref[qt * tq:(qt + 1) * tq, :] = (acc[:, :dv] / acc[:, dv:]).astype(o_ref.dtype)


def _attention(q, k, v, *, row0, n_seq, seq, heads, kv_group, dq, dv, tq, tk, name):
    assert row0 % seq == 0 and seq % tq == 0 and seq % tk == 0 and V_EXT == 2 * dv
    qtiles = max(1, min(seq // tq, MAX_STATIC_TICKS // (seq // tk)))
    bq = qtiles * tq
    assert seq % bq == 0
    nq = seq // bq
    q0, s0 = row0 // bq, row0 // seq
    return pl.pallas_call(
        functools.partial(_attn_kernel, tq=tq, tk=tk, dv=dv),
        grid=(n_seq, heads, nq),
        in_specs=[pl.BlockSpec((bq, dq), lambda b, h, i: (q0 + b * nq + i, h)),
                  pl.BlockSpec((seq, dq), lambda b, h, i: (s0 + b, h // kv_group)),
                  pl.BlockSpec((seq, V_EXT), lambda b, h, i: (s0 + b, h // kv_group))],
        out_specs=pl.BlockSpec((bq, dv), lambda b, h, i: (b * nq + i, h)),
        out_shape=jax.ShapeDtypeStruct((n_seq * seq, heads * dv), BF16),
        scratch_shapes=[pltpu.VMEM((2, tq, tk), F32), pltpu.VMEM((2, tq, tk), BF16),
                        pltpu.VMEM((2, tq, 1), F32), pltpu.VMEM((qtiles, tq, 1), F32),
                        pltpu.VMEM((qtiles, tq, V_EXT), F32)],
        compiler_params=_params(("arbitrary", "arbitrary", "arbitrary")),
        name=name,
    )(q, k, v)


def _outproj_kernel(x_ref, om_ref, og_ref, g_om_ref, g_og_ref, w_o_ref, ln_g_ref, ln_b_ref,
                    x1_ref, x1p_ref):
    om = _rms(om_ref[...].astype(F32), g_om_ref[...]).astype(BF16)
    og = _rms(og_ref[...].astype(F32), g_og_ref[...]).astype(BF16)
    mix = _dot(om, w_o_ref[:MLA_WIDTH, :]) + _dot(og, w_o_ref[MLA_WIDTH:, :])
    x1 = _layer_norm(ALPHA * x_ref[...] + mix, ln_g_ref[...], ln_b_ref[...])
    x1_ref[...] = x1
    x1p_ref[...] = _pack_pair(x1[:, :HALF], x1[:, HALF:])


def _outproj(x, o_mla, o_gqa, wl, tm):
    t = x.shape[0]
    row = lambda w: pl.BlockSpec((tm, w), lambda i: (i, 0))
    consts = (wl['g_o_mla'], wl['g_o_gqa'], wl['w_o'], wl['ln1_g'], wl['ln1_b'])
    return pl.pallas_call(
        _outproj_kernel,
        grid=(t // tm,),
        in_specs=[row(D_MODEL), row(MLA_WIDTH), row(GQA_WIDTH)] + [_const_spec(c.shape) for c in consts],
        out_specs=[row(D_MODEL), row(HALF)],
        out_shape=[jax.ShapeDtypeStruct((t, D_MODEL), F32), jax.ShapeDtypeStruct((t, HALF), U32)],
        compiler_params=_params(("arbitrary",)),
        name="outproj",
    )(x, o_mla, o_gqa, *consts)


def _beats(a, ia, b, ib):
    return (a > b) | ((a == b) & (ia < ib))


def _route_kernel(x_ref, w_rt_ref, bias_ref, mask_ref, gate_ref, counts_ref):
    logits = lax.dot_general(w_rt_ref[...], x_ref[...], (((1,), (1,)), ((), ())),
                             precision=lax.Precision.HIGHEST, preferred_element_type=F32)
    scores = jax.nn.sigmoid(logits)
    sel = scores + bias_ref[...]
    tm = sel.shape[1]
    neg = jnp.float32(-jnp.inf)

    sub = lax.broadcasted_iota(I32, (EXPERTS_PER_GROUP, tm), 0)
    grp_rows = []
    for g in range(N_GROUPS):
        blk = sel[g * EXPERTS_PER_GROUP:(g + 1) * EXPERTS_PER_GROUP, :]
        m1 = jnp.max(blk, axis=0, keepdims=True)
        first = jnp.min(jnp.where(blk == m1, sub, EXPERTS_PER_GROUP), axis=0, keepdims=True)
        m2 = jnp.max(jnp.where(sub == first, neg, blk), axis=0, keepdims=True)
        grp_rows.append(m1 + m2)
    grp = jnp.concatenate(grp_rows, axis=0)

    gidx = lax.broadcasted_iota(I32, (N_GROUPS, tm), 0)
    grank = jnp.zeros((N_GROUPS, tm), I32)
    for g in range(N_GROUPS):
        grank += _beats(grp[g:g + 1, :], g, grp, gidx).astype(I32)
    gkeep = grank < TOPK_GROUPS
    keep = jnp.concatenate(
        [jnp.broadcast_to(gkeep[g:g + 1, :], (EXPERTS_PER_GROUP, tm)) for g in range(N_GROUPS)], axis=0)
    sel = jnp.where(keep, sel, neg)

    eidx = lax.broadcasted_iota(I32, (N_EXPERTS, tm), 0)
    erank = jnp.zeros((N_EXPERTS, tm), I32)
    for e in range(N_EXPERTS):
        erank += _beats(sel[e:e + 1, :], e, sel, eidx).astype(I32)
    chosen = (erank < TOP_K) & keep

    picked = jnp.where(chosen, scores, 0.0)
    gate = picked / jnp.sum(picked, axis=0, keepdims=True) * ROUTED_SCALE
    mask = chosen.astype(F32)
    mask_ref[...] = mask
    gate_ref[...] = gate

    @pl.when(pl.program_id(0) == 0)
    def _():
        counts_ref[...] = jnp.zeros_like(counts_ref)
    counts_ref[...] += jnp.broadcast_to(jnp.sum(mask, axis=1, keepdims=True), counts_ref.shape)


def _route(x1, w_rt, bias, tm):
    t = x1.shape[0]
    col = pl.BlockSpec((N_EXPERTS, tm), lambda i: (0, i))
    return pl.pallas_call(
        _route_kernel,
        grid=(t // tm,),
        in_specs=[pl.BlockSpec((tm, D_MODEL), lambda i: (i, 0)),
                  _const_spec(w_rt.shape), _const_spec(bias.shape)],
        out_specs=[col, col, pl.BlockSpec((N_EXPERTS, LANES), lambda i: (0, 0))],
        out_shape=[jax.ShapeDtypeStruct((N_EXPERTS, t), F32), jax.ShapeDtypeStruct((N_EXPERTS, t), F32),
                   jax.ShapeDtypeStruct((N_EXPERTS, LANES), F32)],
        compiler_params=_params(("arbitrary",)),
        name="route",
    )(x1, w_rt, bias)


def _slots_kernel(mask_ref, gate_ref, start_ref, slot_ref, w_ref, seen_ref):
    @pl.when(pl.program_id(0) == 0)
    def _():
        seen_ref[...] = jnp.zeros_like(seen_ref)

    mask = mask_ref[...]
    tm = mask.shape[1]
    mb = mask.astype(BF16)
    before = (lax.broadcasted_iota(I32, (tm, tm), 0) < lax.broadcasted_iota(I32, (tm, tm), 1)).astype(BF16)
    pos = _dot(mb, before)
    lower = (lax.broadcasted_iota(I32, (N_EXPERTS, N_EXPERTS), 1)
             < lax.broadcasted_iota(I32, (N_EXPERTS, N_EXPERTS), 0)).astype(BF16)
    rank = _dot(lower, mb)
    seen = seen_ref[:, 0:1]
    slot = start_ref[...] + seen + pos
    on = mask > 0.5
    gate = gate_ref[...]
    for k in range(TOP_K):
        pick = on & (rank == float(k))
        slot_ref[k:k + 1, :] = jnp.sum(jnp.where(pick, slot, 0.0), axis=0, keepdims=True).astype(I32)
        w_ref[k:k + 1, :] = jnp.sum(jnp.where(pick, gate, 0.0), axis=0, keepdims=True)
    seen_ref[...] += jnp.broadcast_to(jnp.sum(mask, axis=1, keepdims=True), seen_ref.shape)


def _slots(mask, gate, start, tm):
    t = mask.shape[1]
    col = pl.BlockSpec((N_EXPERTS, tm), lambda i: (0, i))
    out = pl.BlockSpec((TOP_K, tm), lambda i: (0, i))
    return pl.pallas_call(
        _slots_kernel,
        grid=(t // tm,),
        in_specs=[col, col, _const_spec(start.shape)],
        out_specs=[out, out],
        out_shape=[jax.ShapeDtypeStruct((TOP_K, t), I32), jax.ShapeDtypeStruct((TOP_K, t), F32)],
        scratch_shapes=[pltpu.VMEM((N_EXPERTS, LANES), F32)],
        compiler_params=_params(("arbitrary",)),
        name="slots",
    )(mask, gate, start)


def _dispatch_kernel(fill_start_ref, fill_n_ref, slot_ref, x_ref, y_hbm, zero_ref, sem, fill_sem, *, tm):
    def issue(t, c):
        for k in range(TOP_K):
            slot = slot_ref[0, 0, t * TOP_K + k]
            pltpu.make_async_copy(x_ref.at[pl.ds(t, 1)], y_hbm.at[pl.ds(slot, 1)], sem).start(priority=k % 2)
        return c
    lax.fori_loop(0, tm, issue, 0)

    @pl.when(pl.program_id(0) == 0)
    def _():
        zero_ref[...] = jnp.zeros_like(zero_ref)

        def fill_copy(r):
            return pltpu.make_async_copy(zero_ref, y_hbm.at[pl.ds(r, 1)], fill_sem)

        def per_expert(e, c):
            lax.fori_loop(0, fill_n_ref[e], lambda j, c2: (fill_copy(fill_start_ref[e] + j).start(), c2)[1], 0)
            lax.fori_loop(0, fill_n_ref[e], lambda j, c2: (fill_copy(fill_start_ref[e] + j).wait(), c2)[1], 0)
            return c
        lax.fori_loop(0, N_EXPERTS, per_expert, 0)

    rows = y_hbm.at[pl.ds(0, TOP_K * tm)]
    pltpu.make_async_copy(rows, rows, sem).wait()


def _dispatch(x1p, slot, fill_start, fill_n, n_slots, tm):
    t = x1p.shape[0]
    assert TOP_K * tm <= n_slots
    return pl.pallas_call(
        functools.partial(_dispatch_kernel, tm=tm),
        grid_spec=pltpu.PrefetchScalarGridSpec(
            num_scalar_prefetch=2,
            grid=(t // tm,),
            in_specs=[pl.BlockSpec((1, 1, tm * TOP_K), lambda i, *_: (i, 0, 0), memory_space=pltpu.SMEM),
                      pl.BlockSpec((tm, HALF), lambda i, *_: (i, 0))],
            out_specs=pl.BlockSpec(memory_space=pl.ANY),
            scratch_shapes=[pltpu.VMEM((1, HALF), U32), pltpu.SemaphoreType.DMA, pltpu.SemaphoreType.DMA]),
        out_shape=jax.ShapeDtypeStruct((n_slots, HALF), U32),
        compiler_params=pltpu.CompilerParams(dimension_semantics=("arbitrary",), has_side_effects=True),
        name="dispatch",
    )(fill_start, fill_n, slot, x1p)


def _experts_kernel(blk_e_ref, n_used_ref, y_ref, wg_ref, wu_ref, wd_ref, o_ref, wg_s, wu_s, wd_s):
    b = pl.program_id(0)

    @pl.when(b < n_used_ref[0])
    def _():
        @pl.when((b == 0) | (blk_e_ref[b] != blk_e_ref[jnp.maximum(b - 1, 0)]))
        def _():
            wg_s[...] = wg_ref[0, 0].astype(BF16)
            wu_s[...] = wu_ref[0, 0].astype(BF16)
            wd_s[...] = wd_ref[0, 0].astype(BF16)

        lo, hi = _unpack_pair(y_ref[...])
        x = jnp.concatenate([lo.astype(BF16), hi.astype(BF16)], axis=1)
        hid = (jax.nn.silu(_dot(x, wg_s[...])) * _dot(x, wu_s[...])).astype(BF16)
        out = _dot(hid, wd_s[...])
        packed = _pack_pair(out[:, :HALF], out[:, HALF:])
        for j in range(ROW_TILE):
            o_ref[pl.ds(j, packed.shape[0], stride=ROW_TILE), :] = packed[:, j * LANES:(j + 1) * LANES]


def _experts(y_in, blk_e, n_used, w_gate, w_up, w_down, layer, bm):
    n_slots = y_in.shape[0]
    blk = lambda b, be, nu: (jnp.minimum(b, nu[0] - 1), 0)
    wsel = lambda b, be, nu: (layer, be[b], 0, 0)
    return pl.pallas_call(
        _experts_kernel,
        grid_spec=pltpu.PrefetchScalarGridSpec(
            num_scalar_prefetch=2,
            grid=(n_slots // bm,),
            in_specs=[pl.BlockSpec((bm, HALF), blk),
                      pl.BlockSpec((1, 1, D_MODEL, D_EXPERT), wsel),
                      pl.BlockSpec((1, 1, D_MODEL, D_EXPERT), wsel),
                      pl.BlockSpec((1, 1, D_EXPERT, D_MODEL), wsel)],
            out_specs=pl.BlockSpec((bm * ROW_TILE, LANES), blk),
            scratch_shapes=[pltpu.VMEM((D_MODEL, D_EXPERT), BF16), pltpu.VMEM((D_MODEL, D_EXPERT), BF16),
                            pltpu.VMEM((D_EXPERT, D_MODEL), BF16)]),
        out_shape=jax.ShapeDtypeStruct((n_slots * ROW_TILE, LANES), U32),
        compiler_params=_params(("arbitrary",)),
        name="experts",
    )(blk_e, n_used, y_in, w_gate, w_up, w_down)


def _final_kernel(slot_ref, slot_next_ref, y_hbm, w_ref, x1_ref, p_ref, ws_gate_ref, ws_up_ref, ws_down_ref,
                  w_pg_ref, w_ple_ref, ln_g_ref, ln_b_ref, o_ref, rows_ref, sem, *, tm):
    i = pl.program_id(0)
    cur = i % 2

    def gather(ids_ref, buf):
        def issue(t, c):
            dst_row = pl.multiple_of(t * ROW_TILE, ROW_TILE)
            for k in range(TOP_K):
                src_row = pl.multiple_of(ids_ref[0, 0, t * TOP_K + k] * ROW_TILE, ROW_TILE)
                pltpu.make_async_copy(y_hbm.at[pl.ds(src_row, ROW_TILE)],
                                      rows_ref.at[buf, k, pl.ds(dst_row, ROW_TILE)], sem.at[buf]).start()
            return c
        lax.fori_loop(0, tm, issue, 0)

    @pl.when(i == 0)
    def _():
        gather(slot_ref, 0)

    has_next = i + 1 < pl.num_programs(0)
    for buf in range(2):
        @pl.when(has_next & (cur != buf))
        def _():
            gather(slot_next_ref, buf)

    x1 = x1_ref[...]
    xb = x1.astype(BF16)
    hid = (jax.nn.silu(_dot(xb, ws_gate_ref[...])) * _dot(xb, ws_up_ref[...])).astype(BF16)
    shared = _dot(hid, ws_down_ref[...])
    ple = jax.nn.sigmoid(_dot(xb, w_pg_ref[...])) * _dot(p_ref[...].astype(BF16), w_ple_ref[...])
    base = ALPHA * x1 + shared + ple

    pltpu.make_async_copy(rows_ref.at[cur], rows_ref.at[cur], sem.at[cur]).wait()

    w = w_ref[...]
    wk = [jnp.broadcast_to(w[:, k:k + 1], (tm, LANES)) for k in range(TOP_K)]
    r_lo, r_hi = [], []
    for j in range(ROW_TILE):
        a_lo = jnp.zeros((tm, LANES), F32)
        a_hi = jnp.zeros((tm, LANES), F32)
        for k in range(TOP_K):
            lo, hi = _unpack_pair(rows_ref[cur, k, pl.ds(j, tm, stride=ROW_TILE), :])
            a_lo += wk[k] * lo
            a_hi += wk[k] * hi
        r_lo.append(a_lo)
        r_hi.append(a_hi)
    y = base + jnp.concatenate(r_lo + r_hi, axis=1)
    o_ref[...] = _layer_norm(y, ln_g_ref[...], ln_b_ref[...])


def _final(slot, y_out, w_tok, x1, p, wl, tm, tile0, n):
    row = lambda w: pl.BlockSpec((tm, w), lambda i: (tile0 + i, 0))
    ids = lambda index: pl.BlockSpec((1, 1, tm * TOP_K), index, memory_space=pltpu.SMEM)
    consts = (wl['ws_gate'], wl['ws_up'], wl['ws_down'], wl['w_ple_gate'], wl['w_ple'], wl['ln2_g'], wl['ln2_b'])
    return pl.pallas_call(
        functools.partial(_final_kernel, tm=tm),
        grid=(n,),
        in_specs=[ids(lambda i: (tile0 + i, 0, 0)), ids(lambda i: (tile0 + jnp.minimum(i + 1, n - 1), 0, 0)),
                  pl.BlockSpec(memory_space=pl.ANY),
                  row(TOP_K), row(D_MODEL), row(PLE_DIM)] + [_const_spec(c.shape) for c in consts],
        out_specs=pl.BlockSpec((tm, D_MODEL), lambda i: (i, 0)),
        out_shape=jax.ShapeDtypeStruct((n * tm, D_MODEL), F32),
        scratch_shapes=[pltpu.VMEM((2, TOP_K, tm * ROW_TILE, LANES), U32), pltpu.SemaphoreType.DMA((2,))],
        compiler_params=_params(("arbitrary",)),
        name="final",
    )(slot, slot, y_out, w_tok, x1, p, *consts)


def _rope_tables(n):
    t = jnp.arange(n, dtype=F32)
    inv = ROPE_THETA ** (-jnp.arange(0, MLA_ROPE, 2, dtype=F32) / MLA_ROPE)
    a_seq = t[:, None] * inv[None, :]
    a_row = jnp.floor(t / GRID_W)[:, None] * inv[None, :]
    a_col = (t - jnp.floor(t / GRID_W) * GRID_W)[:, None] * inv[None, :]
    z = jnp.zeros((n, LANES // 2), F32)
    cos_m = jnp.concatenate([jnp.cos(a_seq), jnp.cos(a_seq), z], axis=1)
    sin_m = jnp.concatenate([-jnp.sin(a_seq), jnp.sin(a_seq), z], axis=1)
    cos_a = jnp.concatenate([jnp.cos(a_row), jnp.cos(a_row), jnp.cos(a_col), jnp.cos(a_col)], axis=1)
    sin_a = jnp.concatenate([-jnp.sin(a_row), jnp.sin(a_row), -jnp.sin(a_col), jnp.sin(a_col)], axis=1)
    return cos_m, sin_m, cos_a, sin_a


def _layer_weights(i, w_in, g_q_lat, g_kv_lat, w_uq, w_uk, w_uv, g_q_gqa, g_k_gqa, g_o_mla, g_o_gqa,
                   w_o, ln1_g, ln1_b, w_router, router_bias, w_gate, w_up, w_down, ws_gate, ws_up,
                   ws_down, w_ple_gate, w_ple, ln2_g, ln2_b):
    wi = w_in[i]
    kr0 = Q_LORA + KV_LORA
    w_in_p = jnp.concatenate(
        [wi[:, :kr0 + MLA_ROPE], jnp.zeros((D_MODEL, LANES - MLA_ROPE), F32), wi[:, kr0 + MLA_ROPE:]], axis=1)
    uq = w_uq[i].reshape(Q_LORA, MLA_HEADS, MLA_NOPE + MLA_ROPE)
    uq = jnp.pad(uq, ((0, 0), (0, 0), (0, MLA_QK_PAD - MLA_NOPE - MLA_ROPE)))
    vec = lambda a: a[i].reshape(1, -1).astype(F32)
    return dict(
        w_in_p=w_in_p.astype(BF16), w_uq_p=uq.reshape(Q_LORA, MLA_HEADS * MLA_QK_PAD).astype(BF16),
        w_uk=w_uk[i].astype(BF16), w_uv=w_uv[i].astype(BF16),
        g_q_lat=vec(g_q_lat), g_kv_lat=vec(g_kv_lat), g_q_gqa=vec(g_q_gqa), g_k_gqa=vec(g_k_gqa),
        g_o_mla=vec(g_o_mla), g_o_gqa=vec(g_o_gqa), w_o=w_o[i].astype(BF16),
        ln1_g=vec(ln1_g), ln1_b=vec(ln1_b),
        w_rt=w_router[i].T.astype(F32), bias=router_bias[i].reshape(-1, 1).astype(F32),
        ws_gate=ws_gate[i].astype(BF16), ws_up=ws_up[i].astype(BF16), ws_down=ws_down[i].astype(BF16),
        w_ple_gate=w_ple_gate[i].astype(BF16), w_ple=w_ple[i].astype(BF16),
        ln2_g=vec(ln2_g), ln2_b=vec(ln2_b))


def _slot_plan(counts, bm, n_blocks):
    counts = counts.astype(I32)
    padded = (counts + bm - 1) // bm * bm
    pad_end = jnp.cumsum(padded)
    pad_start = pad_end - padded
    first_row = jnp.arange(n_blocks, dtype=I32) * bm
    blk_e = jnp.minimum(jnp.sum(pad_end[None, :] <= first_row[:, None], axis=1), N_EXPERTS - 1).astype(I32)
    n_used = jnp.maximum(pad_end[-1:] // bm, 1).astype(I32)
    return pad_start, pad_start + counts, padded - counts, blk_e, n_used


def kernel(x_prompt, x_sample, p_prompt, p_sample, w_in, g_q_lat, g_kv_lat, w_uq, w_uk, w_uv,
           g_q_gqa, g_k_gqa, g_o_mla, g_o_gqa, w_o, ln1_g, ln1_b, w_router, router_bias,
           w_gate, w_up, w_down, ws_gate, ws_up, ws_down, w_ple_gate, w_ple, ln2_g, ln2_b):
    weights = (w_in, g_q_lat, g_kv_lat, w_uq, w_uk, w_uv, g_q_gqa, g_k_gqa, g_o_mla, g_o_gqa, w_o,
               ln1_g, ln1_b, w_router, router_bias, w_gate, w_up, w_down, ws_gate, ws_up, ws_down,
               w_ple_gate, w_ple, ln2_g, ln2_b)
    b1, s1, _ = x_prompt.shape
    b2, s2, _ = x_sample.shape
    t1, t2 = b1 * s1, b2 * s2
    t = t1 + t2
    x = jnp.concatenate([x_prompt.reshape(t1, D_MODEL), x_sample.reshape(t2, D_MODEL)], axis=0)
    p = jnp.concatenate([p_prompt.reshape(DEPTH, t1, PLE_DIM), p_sample.reshape(DEPTH, t2, PLE_DIM)], axis=1)

    tm = _tile(min(s1, s2), 512)
    tm_final = _tile(min(s1, s2), 256)
    bm =_tile(t * TOP_K // N_EXPERTS, 512)
    n_blocks = t * TOP_K // bm + N_EXPERTS
    tables = _rope_tables(max(s1, s2))

    def pos_block(i):
        return jnp.where(i < t1 // tm, i % (s1 // tm), (i - t1 // tm) % (s2 // tm))

    for layer in range(DEPTH):
        wl = _layer_weights(layer, *weights)
        q_mla, k_mla, v_mla, q_gqa, k_gqa, v_gqa = _proj(x, wl, tables, pos_block, tm)
        groups = ((0, b1, s1), (t1, b2, s2))
        o_mla = jnp.concatenate([
            _attention(q_mla, k_mla, v_mla, row0=r0, n_seq=nb, seq=s, heads=MLA_HEADS, kv_group=1,
                       dq=MLA_QK_PAD, dv=MLA_V, tq=_tile(s, 512), tk=_tile(s, 2048), name="attn_mla")
            for r0, nb, s in groups], axis=0)
        o_gqa = jnp.concatenate([
            _attention(q_gqa, k_gqa, v_gqa, row0=r0, n_seq=nb, seq=s, heads=GQA_HEADS, kv_group=GQA_GROUP,
                       dq=GQA_HD, dv=GQA_HD, tq=_tile(s, 512), tk=_tile(s, 2048), name="attn_gqa")
            for r0, nb, s in groups], axis=0)
        x1, x1p = _outproj(x, o_mla, o_gqa, wl, tm)

        mask, gate, counts = _route(x1, wl['w_rt'], wl['bias'], tm)
        pad_start, fill_start, fill_n, blk_e, n_used = _slot_plan(counts[:, 0], bm, n_blocks)
        slot, w_k = _slots(mask, gate, pad_start.astype(F32).reshape(-1, 1), tm)
        slot_tok = slot.T
        y_in = _dispatch(x1p, slot_tok.reshape(t // tm, 1, tm * TOP_K), fill_start, fill_n, n_blocks * bm, tm)
        y_out = _experts(y_in, blk_e, n_used, w_gate, w_up, w_down, layer, bm)
        final = functools.partial(_final, slot_tok.reshape(t // tm_final, 1, tm_final * TOP_K), y_out, w_k.T, x1,
                                  p[layer], wl, tm_final)
        if layer < DEPTH - 1:
            x = final(0, t // tm_final)
        else:
            y_prompt = final(0, t1 // tm_final)
            y_sample = final(t1 // tm_final, t2 // tm_final)

    return y_prompt.reshape(b1, s1, D_MODEL), y_sample.reshape(b2, s2, D_MODEL)
```

```python
import functools

import jax
import jax.numpy as jnp
from jax import lax
from jax.experimental import pallas as pl
from jax.experimental.pallas import tpu as pltpu

F32 = jnp.float32
BF16 = jnp.bfloat16
U32 = jnp.uint32
I32 = jnp.int32

D_MODEL = 2048
DEPTH = 2
GRID_W = 64
ROPE_THETA = 10000.0
MLA_HEADS = 8
MLA_NOPE = 128
MLA_ROPE = 64
MLA_V = 128
MLA_QK_PAD = 256
Q_LORA = 512
KV_LORA = 512
MLA_WIDTH = MLA_HEADS * MLA_V
MLA_SCALE = (MLA_NOPE + MLA_ROPE) ** -0.5
GQA_HD = 128
GQA_HEADS = 8
GQA_KV_HEADS = 2
GQA_GROUP = GQA_HEADS // GQA_KV_HEADS
GQA_WIDTH = GQA_HEADS * GQA_HD
GQA_SCALE = GQA_HD ** -0.5
N_EXPERTS = 64
TOP_K = 8
N_GROUPS = 8
TOPK_GROUPS = 4
EXPERTS_PER_GROUP = N_EXPERTS // N_GROUPS
D_EXPERT = 512
ROUTED_SCALE = 2.5
PLE_DIM = 256
ALPHA = (2 * DEPTH) ** 0.25
RMS_EPS = 1e-6
LN_EPS = 1e-5
V_EXT = 256
LOG2E = 1.4426950408889634
MAX_STATIC_TICKS = 8
TICKS_PER_TRIP = 2
HALF = D_MODEL // 2

LANES = 128
ROW_TILE = 8
VMEM_LIMIT = 56 * 1024 * 1024

C_CQ = 0
C_CKV = C_CQ + Q_LORA
C_KR = C_CKV + KV_LORA
C_QG = C_KR + LANES
C_KG = C_QG + GQA_WIDTH
C_VG = C_KG + GQA_KV_HEADS * GQA_HD
IN_COLS_PAD = C_VG + GQA_KV_HEADS * GQA_HD


def _tile(n, pref):
    t = min(n, pref)
    while n % t:
        t //= 2
    return t


def _const_spec(shape):
    nd = len(shape)
    return pl.BlockSpec(shape, lambda *_: (0,) * nd, pipeline_mode=pl.Buffered(1))


def _params(sem):
    return pltpu.CompilerParams(dimension_semantics=sem, vmem_limit_bytes=VMEM_LIMIT)


def _pack_pair(lo, hi):
    ulo = lax.bitcast_convert_type(lo.astype(BF16).astype(F32), U32)
    uhi = lax.bitcast_convert_type(hi.astype(BF16).astype(F32), U32)
    return (ulo >> 16) | uhi


def _unpack_pair(u):
    lo = lax.bitcast_convert_type(u << 16, F32)
    hi = lax.bitcast_convert_type(u & jnp.uint32(0xFFFF0000), F32)
    return lo, hi


def _store_row_tiles(ref, packed):
    for j in range(ROW_TILE):
        ref[pl.ds(j, packed.shape[0], stride=ROW_TILE), :] = packed[:, j * LANES:(j + 1) * LANES]


def _load_row_tiles(ref, n):
    return [ref[pl.ds(j, n, stride=ROW_TILE), :] for j in range(ROW_TILE)]


def _rot_half_partner(x):
    lane = lax.broadcasted_iota(I32, x.shape, x.ndim - 1)
    n = x.shape[-1]
    up = pltpu.roll(x, n - 32, x.ndim - 1)
    down = pltpu.roll(x, 32, x.ndim - 1)
    return jnp.where((lane & 32) == 0, up, down)


def _rope(x, cos, sin_signed):
    return x * cos + _rot_half_partner(x) * sin_signed


def _rms(x, g):
    return x * lax.rsqrt(jnp.mean(x * x, axis=-1, keepdims=True) + RMS_EPS) * g


def _layer_norm(x, g, b):
    mu = jnp.mean(x, axis=-1, keepdims=True)
    xc = x - mu
    var = jnp.mean(xc * xc, axis=-1, keepdims=True)
    return xc * lax.rsqrt(var + LN_EPS) * g + b


def _dot(a, b):
    return jnp.dot(a, b, preferred_element_type=F32)


def _proj_kernel(x_ref, w_in_ref, w_uq_ref, w_uk_ref, w_uv_ref, g_q_lat_ref, g_kv_lat_ref,
                 g_q_gqa_ref, g_k_gqa_ref, cos_m_ref, sin_m_ref, cos_a_ref, sin_a_ref,
                 q_mla_ref, k_mla_ref, v_mla_ref, q_gqa_ref, k_gqa_ref, v_gqa_ref):
    xb = x_ref[...].astype(BF16)
    cos_m, sin_m = cos_m_ref[...], sin_m_ref[...]
    cos_a, sin_a = cos_a_ref[...], sin_a_ref[...]

    lat = _dot(xb, w_in_ref[:, C_CQ:C_QG])

    cq = _rms(lat[:, C_CQ:C_CQ + Q_LORA], g_q_lat_ref[...]).astype(BF16)
    q_all = _dot(cq, w_uq_ref[...])
    for h in range(MLA_HEADS):
        c0, c1 = h * MLA_QK_PAD, h * MLA_QK_PAD + MLA_NOPE
        q_rope = _rope(q_all[:, c1:c0 + MLA_QK_PAD], cos_m, sin_m)
        q_mla_ref[:, c0:c1] = (q_all[:, c0:c1] * (MLA_SCALE * LOG2E)).astype(BF16)
        q_mla_ref[:, c1:c0 + MLA_QK_PAD] = (q_rope * (MLA_SCALE * LOG2E)).astype(BF16)

    ckv = _rms(lat[:, C_CKV:C_CKV + KV_LORA], g_kv_lat_ref[...]).astype(BF16)
    k_rope = _rope(lat[:, C_KR:C_KR + LANES], cos_m, sin_m).astype(BF16)
    k_all = _dot(ckv, w_uk_ref[...]).astype(BF16)
    v_all = _dot(ckv, w_uv_ref[...]).astype(BF16)
    ones = jnp.ones((xb.shape[0], V_EXT - MLA_V), BF16)
    for h in range(MLA_HEADS):
        c0 = h * MLA_QK_PAD
        k_mla_ref[:, c0:c0 + MLA_NOPE] = k_all[:, h * MLA_NOPE:(h + 1) * MLA_NOPE]
        k_mla_ref[:, c0 + MLA_NOPE:c0 + MLA_QK_PAD] = k_rope
        v_mla_ref[:, h * V_EXT:h * V_EXT + MLA_V] = v_all[:, h * MLA_V:(h + 1) * MLA_V]
        v_mla_ref[:, h * V_EXT + MLA_V:(h + 1) * V_EXT] = ones

    g_q, g_k = g_q_gqa_ref[...], g_k_gqa_ref[...]
    qg_all = _dot(xb, w_in_ref[:, C_QG:C_KG])
    for h in range(GQA_HEADS):
        qh = _rope(_rms(qg_all[:, h * GQA_HD:(h + 1) * GQA_HD], g_q), cos_a, sin_a) * (GQA_SCALE * LOG2E)
        q_gqa_ref[:, h * GQA_HD:(h + 1) * GQA_HD] = qh.astype(BF16)
    kv_all = _dot(xb, w_in_ref[:, C_KG:IN_COLS_PAD])
    for h in range(GQA_KV_HEADS):
        kh = kv_all[:, h * GQA_HD:(h + 1) * GQA_HD]
        k_gqa_ref[:, h * GQA_HD:(h + 1) * GQA_HD] = _rope(_rms(kh, g_k), cos_a, sin_a).astype(BF16)
        vh = kv_all[:, C_VG - C_KG + h * GQA_HD:C_VG - C_KG + (h + 1) * GQA_HD]
        v_gqa_ref[:, h * V_EXT:h * V_EXT + GQA_HD] = vh.astype(BF16)
        v_gqa_ref[:, h * V_EXT + GQA_HD:(h + 1) * V_EXT] = ones


def _proj(x, wl, tables, pos_block, tm):
    t = x.shape[0]
    row = lambda w: pl.BlockSpec((tm, w), lambda i: (i, 0))
    tab = pl.BlockSpec((tm, LANES), lambda i: (pos_block(i), 0))
    widths = (MLA_HEADS * MLA_QK_PAD, MLA_HEADS * MLA_QK_PAD, MLA_HEADS * V_EXT,
              GQA_WIDTH, GQA_KV_HEADS * GQA_HD, GQA_KV_HEADS * V_EXT)
    consts = (wl['w_in_p'], wl['w_uq_p'], wl['w_uk'], wl['w_uv'],
              wl['g_q_lat'], wl['g_kv_lat'], wl['g_q_gqa'], wl['g_k_gqa'])
    return pl.pallas_call(
        _proj_kernel,
        grid=(t // tm,),
        in_specs=[row(D_MODEL)] + [_const_spec(c.shape) for c in consts] + [tab] * 4,
        out_specs=[row(w) for w in widths],
        out_shape=[jax.ShapeDtypeStruct((t, w), BF16) for w in widths],
        compiler_params=_params(("arbitrary",)),
        name="proj",
    )(x, *consts, *tables)


def _attn_kernel(q_ref, k_ref, v_ref, o_ref, s_ref, p_ref, a_ref, m_ref, acc_ref, *, tq, tk, dv):
    nq = q_ref.shape[0] // tq
    nk = k_ref.shape[0] // tk
    n_items = nq * nk

    def split(i):
        return (0, i) if nq == 1 else (i // nk, i % nk)

    def scores(i, slot):
        qt, c = split(i)
        off = pl.multiple_of(c * tk, tk)
        s_ref[slot] = lax.dot_general(q_ref[pl.ds(qt * tq, tq), :], k_ref[pl.ds(off, tk), :],
                                      (((1,), (1,)), ((), ())), preferred_element_type=F32)

    def softmax(i, slot):
        qt, _ = split(i)
        s = s_ref[slot]
        m = m_ref[qt]
        m_new = jnp.maximum(m, jnp.max(s, axis=1, keepdims=True))
        a_ref[slot] = jnp.exp2(m - m_new)
        p_ref[slot] = jnp.exp2(s - m_new).astype(BF16)
        m_ref[qt] = m_new

    def values(i, slot):
        qt, c = split(i)
        off = pl.multiple_of(c * tk, tk)
        acc_ref[qt] = a_ref[slot] * acc_ref[qt] + _dot(p_ref[slot], v_ref[pl.ds(off, tk), :])

    def tick(i, parity, first=False, last=False):
        if not last:
            scores(i + 1, 1 - parity)
        if not first:
            values(i - 1, 1 - parity)
        softmax(i, parity)

    m_ref[...] = jnp.full(m_ref.shape, -jnp.inf, F32)
    acc_ref[...] = jnp.zeros(acc_ref.shape, F32)
    scores(0, 0)
    if n_items <= MAX_STATIC_TICKS:
        for i in range(n_items):
            tick(i, i % 2, first=i == 0, last=i == n_items - 1)
    else:
        assert nq == 1 and nk % 2 == 0
        tick(0, 0, first=True)
        n_loop = (nk - 2) // TICKS_PER_TRIP

        def trip(j, c):
            i = TICKS_PER_TRIP * j + 1
            for d in range(TICKS_PER_TRIP):
                tick(i + d, (1 + d) % 2)
            return c
        lax.fori_loop(0, n_loop, trip, 0)
        for i in range(1 + n_loop * TICKS_PER_TRIP, nk - 1):
            tick(i, i % 2)
        tick(nk - 1, 1, last=True)
    values(n_items - 1, (n_items - 1) % 2)
    for qt in range(nq):
        acc = acc_ref[qt]
        o_ref[qt * tq:(qt + 1) * tq, :] = (acc[:, :dv] / acc[:, dv:]).astype(o_ref.dtype)


def _attention(q, k, v, *, row0, n_seq, seq, heads, kv_group, dq, dv, tq, tk, name):
    assert row0 % seq == 0 and seq % tq == 0 and seq % tk == 0 and V_EXT == 2 * dv
    qtiles = max(1, min(seq // tq, MAX_STATIC_TICKS // (seq // tk)))
    bq = qtiles * tq
    assert seq % bq == 0
    nq = seq // bq
    q0, s0 = row0 // bq, row0 // seq
    return pl.pallas_call(
        functools.partial(_attn_kernel, tq=tq, tk=tk, dv=dv),
        grid=(n_seq, heads, nq),
        in_specs=[pl.BlockSpec((bq, dq), lambda b, h, i: (q0 + b * nq + i, h)),
                  pl.BlockSpec((seq, dq), lambda b, h, i: (s0 + b, h // kv_group)),
                  pl.BlockSpec((seq, V_EXT), lambda b, h, i: (s0 + b, h // kv_group))],
        out_specs=pl.BlockSpec((bq, dv), lambda b, h, i: (b * nq + i, h)),
        out_shape=jax.ShapeDtypeStruct((n_seq * seq, heads * dv), BF16),
        scratch_shapes=[pltpu.VMEM((2, tq, tk), F32), pltpu.VMEM((2, tq, tk), BF16),
                        pltpu.VMEM((2, tq, 1), F32), pltpu.VMEM((qtiles, tq, 1), F32),
                        pltpu.VMEM((qtiles, tq, V_EXT), F32)],
        compiler_params=_params(("arbitrary", "arbitrary", "arbitrary")),
        name=name,
    )(q, k, v)


def _outproj_kernel(x_ref, om_ref, og_ref, g_om_ref, g_og_ref, w_o_ref, ln_g_ref, ln_b_ref,
                    x1_ref, x1p_ref):
    om = _rms(om_ref[...].astype(F32), g_om_ref[...]).astype(BF16)
    og = _rms(og_ref[...].astype(F32), g_og_ref[...]).astype(BF16)
    mix = _dot(om, w_o_ref[:MLA_WIDTH, :]) + _dot(og, w_o_ref[MLA_WIDTH:, :])
    x1 = _layer_norm(ALPHA * x_ref[...] + mix, ln_g_ref[...], ln_b_ref[...])
    x1_ref[...] = x1
    _store_row_tiles(x1p_ref, _pack_pair(x1[:, :HALF], x1[:, HALF:]))


def _outproj(x, o_mla, o_gqa, wl, tm):
    t = x.shape[0]
    row = lambda w: pl.BlockSpec((tm, w), lambda i: (i, 0))
    consts = (wl['g_o_mla'], wl['g_o_gqa'], wl['w_o'], wl['ln1_g'], wl['ln1_b'])
    return pl.pallas_call(
        _outproj_kernel,
        grid=(t // tm,),
        in_specs=[row(D_MODEL), row(MLA_WIDTH), row(GQA_WIDTH)] + [_const_spec(c.shape) for c in consts],
        out_specs=[row(D_MODEL), pl.BlockSpec((tm * ROW_TILE, LANES), lambda i: (i, 0))],
        out_shape=[jax.ShapeDtypeStruct((t, D_MODEL), F32), jax.ShapeDtypeStruct((t * ROW_TILE, LANES), U32)],
        compiler_params=_params(("arbitrary",)),
        name="outproj",
    )(x, o_mla, o_gqa, *consts)


def _beats(a, ia, b, ib):
    return (a > b) | ((a == b) & (ia < ib))


def _route_kernel(x_ref, w_rt_ref, bias_ref, mask_ref, gate_ref, counts_ref):
    logits = lax.dot_general(w_rt_ref[...], x_ref[...], (((1,), (1,)), ((), ())),
                             precision=lax.Precision.HIGHEST, preferred_element_type=F32)
    scores = jax.nn.sigmoid(logits)
    sel = scores + bias_ref[...]
    tm = sel.shape[1]
    neg = jnp.float32(-jnp.inf)

    sub = lax.broadcasted_iota(I32, (EXPERTS_PER_GROUP, tm), 0)
    grp_rows = []
    for g in range(N_GROUPS):
        blk = sel[g * EXPERTS_PER_GROUP:(g + 1) * EXPERTS_PER_GROUP, :]
        m1 = jnp.max(blk, axis=0, keepdims=True)
        first = jnp.min(jnp.where(blk == m1, sub, EXPERTS_PER_GROUP), axis=0, keepdims=True)
        m2 = jnp.max(jnp.where(sub == first, neg, blk), axis=0, keepdims=True)
        grp_rows.append(m1 + m2)
    grp = jnp.concatenate(grp_rows, axis=0)

    gidx = lax.broadcasted_iota(I32, (N_GROUPS, tm), 0)
    grank = jnp.zeros((N_GROUPS, tm), I32)
    for g in range(N_GROUPS):
        grank += _beats(grp[g:g + 1, :], g, grp, gidx).astype(I32)
    gkeep = grank < TOPK_GROUPS
    keep = jnp.concatenate(
        [jnp.broadcast_to(gkeep[g:g + 1, :], (EXPERTS_PER_GROUP, tm)) for g in range(N_GROUPS)], axis=0)
    sel = jnp.where(keep, sel, neg)

    eidx = lax.broadcasted_iota(I32, (N_EXPERTS, tm), 0)
    erank = jnp.zeros((N_EXPERTS, tm), I32)
    for e in range(N_EXPERTS):
        erank += _beats(sel[e:e + 1, :], e, sel, eidx).astype(I32)
    chosen = (erank < TOP_K) & keep

    picked = jnp.where(chosen, scores, 0.0)
    gate = picked / jnp.sum(picked, axis=0, keepdims=True) * ROUTED_SCALE
    mask = chosen.astype(F32)
    mask_ref[...] = mask
    gate_ref[...] = gate

    @pl.when(pl.program_id(0) == 0)
    def _():
        counts_ref[...] = jnp.zeros_like(counts_ref)
    counts_ref[...] += jnp.broadcast_to(jnp.sum(mask, axis=1, keepdims=True), counts_ref.shape)


def _route(x1, w_rt, bias, tm):
    t = x1.shape[0]
    col = pl.BlockSpec((N_EXPERTS, tm), lambda i: (0, i))
    return pl.pallas_call(
        _route_kernel,
        grid=(t // tm,),
        in_specs=[pl.BlockSpec((tm, D_MODEL), lambda i: (i, 0)),
                  _const_spec(w_rt.shape), _const_spec(bias.shape)],
        out_specs=[col, col, pl.BlockSpec((N_EXPERTS, LANES), lambda i: (0, 0))],
        out_shape=[jax.ShapeDtypeStruct((N_EXPERTS, t), F32), jax.ShapeDtypeStruct((N_EXPERTS, t), F32),
                   jax.ShapeDtypeStruct((N_EXPERTS, LANES), F32)],
        compiler_params=_params(("arbitrary",)),
        name="route",
    )(x1, w_rt, bias)


def _slots_kernel(mask_ref, gate_ref, start_ref, slot_ref, w_ref, seen_ref):
    @pl.when(pl.program_id(0) == 0)
    def _():
        seen_ref[...] = jnp.zeros_like(seen_ref)

    mask = mask_ref[...]
    tm = mask.shape[1]
    mb = mask.astype(BF16)
    before = (lax.broadcasted_iota(I32, (tm, tm), 0) < lax.broadcasted_iota(I32, (tm, tm), 1)).astype(BF16)
    pos = _dot(mb, before)
    lower = (lax.broadcasted_iota(I32, (N_EXPERTS, N_EXPERTS), 1)
             < lax.broadcasted_iota(I32, (N_EXPERTS, N_EXPERTS), 0)).astype(BF16)
    rank = _dot(lower, mb)
    seen = seen_ref[:, 0:1]
    slot = start_ref[...] + seen + pos
    on = mask > 0.5
    gate = gate_ref[...]
    for k in range(TOP_K):
        pick = on & (rank == float(k))
        slot_ref[k:k + 1, :] = jnp.sum(jnp.where(pick, slot, 0.0), axis=0, keepdims=True).astype(I32)
        w_ref[k:k + 1, :] = jnp.sum(jnp.where(pick, gate, 0.0), axis=0, keepdims=True)
    seen_ref[...] += jnp.broadcast_to(jnp.sum(mask, axis=1, keepdims=True), seen_ref.shape)


def _slots(mask, gate, start, tm):
    t = mask.shape[1]
    col = pl.BlockSpec((N_EXPERTS, tm), lambda i: (0, i))
    out = pl.BlockSpec((TOP_K, tm), lambda i: (0, i))
    return pl.pallas_call(
        _slots_kernel,
        grid=(t // tm,),
        in_specs=[col, col, _const_spec(start.shape)],
        out_specs=[out, out],
        out_shape=[jax.ShapeDtypeStruct((TOP_K, t), I32), jax.ShapeDtypeStruct((TOP_K, t), F32)],
        scratch_shapes=[pltpu.VMEM((N_EXPERTS, LANES), F32)],
        compiler_params=_params(("arbitrary",)),
        name="slots",
    )(mask, gate, start)


def _dispatch_kernel(fill_start_ref, fill_n_ref, slot_ref, x_ref, y_hbm, zero_ref, sem, fill_sem, *, tm):
    def issue(t, c):
        for k in range(TOP_K):
            slot = slot_ref[0, 0, t * TOP_K + k]
            pltpu.make_async_copy(x_ref.at[pl.ds(pl.multiple_of(t * ROW_TILE, ROW_TILE), ROW_TILE)],
                                  y_hbm.at[pl.ds(pl.multiple_of(slot * ROW_TILE, ROW_TILE), ROW_TILE)],
                                  sem).start(priority=k % 2)
        return c
    lax.fori_loop(0, tm, issue, 0)

    @pl.when(pl.program_id(0) == 0)
    def _():
        zero_ref[...] = jnp.zeros_like(zero_ref)

        def fill_copy(r):
            return pltpu.make_async_copy(zero_ref, y_hbm.at[pl.ds(pl.multiple_of(r * ROW_TILE, ROW_TILE), ROW_TILE)],
                                         fill_sem)

        def per_expert(e, c):
            lax.fori_loop(0, fill_n_ref[e], lambda j, c2: (fill_copy(fill_start_ref[e] + j).start(), c2)[1], 0)
            lax.fori_loop(0, fill_n_ref[e], lambda j, c2: (fill_copy(fill_start_ref[e] + j).wait(), c2)[1], 0)
            return c
        lax.fori_loop(0, N_EXPERTS, per_expert, 0)

    rows = y_hbm.at[pl.ds(0, TOP_K * tm * ROW_TILE)]
    pltpu.make_async_copy(rows, rows, sem).wait()


def _dispatch(x1p, slot, fill_start, fill_n, n_slots, tm):
    t = x1p.shape[0] // ROW_TILE
    assert TOP_K * tm <= n_slots
    return pl.pallas_call(
        functools.partial(_dispatch_kernel, tm=tm),
        grid_spec=pltpu.PrefetchScalarGridSpec(
            num_scalar_prefetch=2,
            grid=(t // tm,),
            in_specs=[pl.BlockSpec((1, 1, tm * TOP_K), lambda i, *_: (i, 0, 0), memory_space=pltpu.SMEM),
                      pl.BlockSpec((tm * ROW_TILE, LANES), lambda i, *_: (i, 0))],
            out_specs=pl.BlockSpec(memory_space=pl.ANY),
            scratch_shapes=[pltpu.VMEM((ROW_TILE, LANES), U32), pltpu.SemaphoreType.DMA, pltpu.SemaphoreType.DMA]),
        out_shape=jax.ShapeDtypeStruct((n_slots * ROW_TILE, LANES), U32),
        compiler_params=pltpu.CompilerParams(dimension_semantics=("arbitrary",), has_side_effects=True),
        name="dispatch",
    )(fill_start, fill_n, slot, x1p)


def _experts_kernel(blk_e_ref, n_used_ref, y_ref, wg_ref, wu_ref, wd_ref, o_ref, wg_s, wu_s, wd_s):
    b = pl.program_id(0)

    @pl.when(b < n_used_ref[0])
    def _():
        @pl.when((b == 0) | (blk_e_ref[b] != blk_e_ref[jnp.maximum(b - 1, 0)]))
        def _():
            wg_s[...] = wg_ref[0, 0].astype(BF16)
            wu_s[...] = wu_ref[0, 0].astype(BF16)
            wd_s[...] = wd_ref[0, 0].astype(BF16)

        bm = y_ref.shape[0] // ROW_TILE
        halves = [_unpack_pair(u) for u in _load_row_tiles(y_ref, bm)]
        x = jnp.concatenate([lo.astype(BF16) for lo, _ in halves] + [hi.astype(BF16) for _, hi in halves], axis=1)
        hid = (jax.nn.silu(_dot(x, wg_s[...])) * _dot(x, wu_s[...])).astype(BF16)
        out = _dot(hid, wd_s[...])
        _store_row_tiles(o_ref, _pack_pair(out[:, :HALF], out[:, HALF:]))


def _experts(y_in, blk_e, n_used, w_gate, w_up, w_down, layer, bm):
    n_slots = y_in.shape[0] // ROW_TILE
    blk = lambda b, be, nu: (jnp.minimum(b, nu[0] - 1), 0)
    wsel = lambda b, be, nu: (layer, be[b], 0, 0)
    return pl.pallas_call(
        _experts_kernel,
        grid_spec=pltpu.PrefetchScalarGridSpec(
            num_scalar_prefetch=2,
            grid=(n_slots // bm,),
            in_specs=[pl.BlockSpec((bm * ROW_TILE, LANES), blk),
                      pl.BlockSpec((1, 1, D_MODEL, D_EXPERT), wsel),
                      pl.BlockSpec((1, 1, D_MODEL, D_EXPERT), wsel),
                      pl.BlockSpec((1, 1, D_EXPERT, D_MODEL), wsel)],
            out_specs=pl.BlockSpec((bm * ROW_TILE, LANES), blk),
            scratch_shapes=[pltpu.VMEM((D_MODEL, D_EXPERT), BF16), pltpu.VMEM((D_MODEL, D_EXPERT), BF16),
                            pltpu.VMEM((D_EXPERT, D_MODEL), BF16)]),
        out_shape=jax.ShapeDtypeStruct((n_slots * ROW_TILE, LANES), U32),
        compiler_params=_params(("arbitrary",)),
        name="experts",
    )(blk_e, n_used, y_in, w_gate, w_up, w_down)


def _final_kernel(slot_ref, slot_next_ref, y_hbm, w_ref, x1_ref, p_ref, ws_gate_ref, ws_up_ref, ws_down_ref,
                  w_pg_ref, w_ple_ref, ln_g_ref, ln_b_ref, o_ref, rows_ref, sem, *, tm):
    i = pl.program_id(0)
    cur = i % 2

    def gather(ids_ref, buf):
        def issue(t, c):
            dst_row = pl.multiple_of(t * ROW_TILE, ROW_TILE)
            for k in range(TOP_K):
                src_row = pl.multiple_of(ids_ref[0, 0, t * TOP_K + k] * ROW_TILE, ROW_TILE)
                pltpu.make_async_copy(y_hbm.at[pl.ds(src_row, ROW_TILE)],
                                      rows_ref.at[buf, k, pl.ds(dst_row, ROW_TILE)], sem.at[buf]).start()
            return c
        lax.fori_loop(0, tm, issue, 0)

    @pl.when(i == 0)
    def _():
        gather(slot_ref, 0)

    has_next = i + 1 < pl.num_programs(0)
    for buf in range(2):
        @pl.when(has_next & (cur != buf))
        def _():
            gather(slot_next_ref, buf)

    x1 = x1_ref[...]
    xb = x1.astype(BF16)
    hid = (jax.nn.silu(_dot(xb, ws_gate_ref[...])) * _dot(xb, ws_up_ref[...])).astype(BF16)
    shared = _dot(hid, ws_down_ref[...])
    ple = jax.nn.sigmoid(_dot(xb, w_pg_ref[...])) * _dot(p_ref[...].astype(BF16), w_ple_ref[...])
    base = ALPHA * x1 + shared + ple

    pltpu.make_async_copy(rows_ref.at[cur], rows_ref.at[cur], sem.at[cur]).wait()

    w = w_ref[...]
    wk = [jnp.broadcast_to(w[:, k:k + 1], (tm, LANES)) for k in range(TOP_K)]
    r_lo, r_hi = [], []
    for j in range(ROW_TILE):
        a_lo = jnp.zeros((tm, LANES), F32)
        a_hi = jnp.zeros((tm, LANES), F32)
        for k in range(TOP_K):
            lo, hi = _unpack_pair(rows_ref[cur, k, pl.ds(j, tm, stride=ROW_TILE), :])
            a_lo += wk[k] * lo
            a_hi += wk[k] * hi
        r_lo.append(a_lo)
        r_hi.append(a_hi)
    y = base + jnp.concatenate(r_lo + r_hi, axis=1)
    o_ref[...] = _layer_norm(y, ln_g_ref[...], ln_b_ref[...])


def _final(slot, y_out, w_tok, x1, p, wl, tm, tile0, n):
    row = lambda w: pl.BlockSpec((tm, w), lambda i: (tile0 + i, 0))
    ids = lambda index: pl.BlockSpec((1, 1, tm * TOP_K), index, memory_space=pltpu.SMEM)
    consts = (wl['ws_gate'], wl['ws_up'], wl['ws_down'], wl['w_ple_gate'], wl['w_ple'], wl['ln2_g'], wl['ln2_b'])
    return pl.pallas_call(
        functools.partial(_final_kernel, tm=tm),
        grid=(n,),
        in_specs=[ids(lambda i: (tile0 + i, 0, 0)), ids(lambda i: (tile0 + jnp.minimum(i + 1, n - 1), 0, 0)),
                  pl.BlockSpec(memory_space=pl.ANY),
                  row(TOP_K), row(D_MODEL), row(PLE_DIM)] + [_const_spec(c.shape) for c in consts],
        out_specs=pl.BlockSpec((tm, D_MODEL), lambda i: (i, 0)),
        out_shape=jax.ShapeDtypeStruct((n * tm, D_MODEL), F32),
        scratch_shapes=[pltpu.VMEM((2, TOP_K, tm * ROW_TILE, LANES), U32), pltpu.SemaphoreType.DMA((2,))],
        compiler_params=_params(("arbitrary",)),
        name="final",
    )(slot, slot, y_out, w_tok, x1, p, *consts)


def _rope_tables(n):
    t = jnp.arange(n, dtype=F32)
    inv = ROPE_THETA ** (-jnp.arange(0, MLA_ROPE, 2, dtype=F32) / MLA_ROPE)
    a_seq = t[:, None] * inv[None, :]
    a_row = jnp.floor(t / GRID_W)[:, None] * inv[None, :]
    a_col = (t - jnp.floor(t / GRID_W) * GRID_W)[:, None] * inv[None, :]
    z = jnp.zeros((n, LANES // 2), F32)
    cos_m = jnp.concatenate([jnp.cos(a_seq), jnp.cos(a_seq), z], axis=1)
    sin_m = jnp.concatenate([-jnp.sin(a_seq), jnp.sin(a_seq), z], axis=1)
    cos_a = jnp.concatenate([jnp.cos(a_row), jnp.cos(a_row), jnp.cos(a_col), jnp.cos(a_col)], axis=1)
    sin_a = jnp.concatenate([-jnp.sin(a_row), jnp.sin(a_row), -jnp.sin(a_col), jnp.sin(a_col)], axis=1)
    return cos_m, sin_m, cos_a, sin_a


def _layer_weights(i, w_in, g_q_lat, g_kv_lat, w_uq, w_uk, w_uv, g_q_gqa, g_k_gqa, g_o_mla, g_o_gqa,
                   w_o, ln1_g, ln1_b, w_router, router_bias, w_gate, w_up, w_down, ws_gate, ws_up,
                   ws_down, w_ple_gate, w_ple, ln2_g, ln2_b):
    wi = w_in[i]
    kr0 = Q_LORA + KV_LORA
    w_in_p = jnp.concatenate(
        [wi[:, :kr0 + MLA_ROPE], jnp.zeros((D_MODEL, LANES - MLA_ROPE), F32), wi[:, kr0 + MLA_ROPE:]], axis=1)
    uq = w_uq[i].reshape(Q_LORA, MLA_HEADS, MLA_NOPE + MLA_ROPE)
    uq = jnp.pad(uq, ((0, 0), (0, 0), (0, MLA_QK_PAD - MLA_NOPE - MLA_ROPE)))
    vec = lambda a: a[i].reshape(1, -1).astype(F32)
    return dict(
        w_in_p=w_in_p.astype(BF16), w_uq_p=uq.reshape(Q_LORA, MLA_HEADS * MLA_QK_PAD).astype(BF16),
        w_uk=w_uk[i].astype(BF16), w_uv=w_uv[i].astype(BF16),
        g_q_lat=vec(g_q_lat), g_kv_lat=vec(g_kv_lat), g_q_gqa=vec(g_q_gqa), g_k_gqa=vec(g_k_gqa),
        g_o_mla=vec(g_o_mla), g_o_gqa=vec(g_o_gqa), w_o=w_o[i].astype(BF16),
        ln1_g=vec(ln1_g), ln1_b=vec(ln1_b),
        w_rt=w_router[i].T.astype(F32), bias=router_bias[i].reshape(-1, 1).astype(F32),
        ws_gate=ws_gate[i].astype(BF16), ws_up=ws_up[i].astype(BF16), ws_down=ws_down[i].astype(BF16),
        w_ple_gate=w_ple_gate[i].astype(BF16), w_ple=w_ple[i].astype(BF16),
        ln2_g=vec(ln2_g), ln2_b=vec(ln2_b))


def _slot_plan(counts, bm, n_blocks):
    counts = counts.astype(I32)
    padded = (counts + bm - 1) // bm * bm
    pad_end = jnp.cumsum(padded)
    pad_start = pad_end - padded
    first_row = jnp.arange(n_blocks, dtype=I32) * bm
    blk_e = jnp.minimum(jnp.sum(pad_end[None, :] <= first_row[:, None], axis=1), N_EXPERTS - 1).astype(I32)
    n_used = jnp.maximum(pad_end[-1:] // bm, 1).astype(I32)
    return pad_start, pad_start + counts, padded - counts, blk_e, n_used


def kernel(x_prompt, x_sample, p_prompt, p_sample, w_in, g_q_lat, g_kv_lat, w_uq, w_uk, w_uv,
           g_q_gqa, g_k_gqa, g_o_mla, g_o_gqa, w_o, ln1_g, ln1_b, w_router, router_bias,
           w_gate, w_up, w_down, ws_gate, ws_up, ws_down, w_ple_gate, w_ple, ln2_g, ln2_b):
    weights = (w_in, g_q_lat, g_kv_lat, w_uq, w_uk, w_uv, g_q_gqa, g_k_gqa, g_o_mla, g_o_gqa, w_o,
               ln1_g, ln1_b, w_router, router_bias, w_gate, w_up, w_down, ws_gate, ws_up, ws_down,
               w_ple_gate, w_ple, ln2_g, ln2_b)
    b1, s1, _ = x_prompt.shape
    b2, s2, _ = x_sample.shape
    t1, t2 = b1 * s1, b2 * s2
    t = t1 + t2
    x = jnp.concatenate([x_prompt.reshape(t1, D_MODEL), x_sample.reshape(t2, D_MODEL)], axis=0)
    p = jnp.concatenate([p_prompt.reshape(DEPTH, t1, PLE_DIM), p_sample.reshape(DEPTH, t2, PLE_DIM)], axis=1)

    tm = _tile(min(s1, s2), 512)
    tm_final = _tile(min(s1, s2), 256)
    bm =_tile(t * TOP_K // N_EXPERTS, 512)
    n_blocks = t * TOP_K // bm + N_EXPERTS
    tables = _rope_tables(max(s1, s2))

    def pos_block(i):
        return jnp.where(i < t1 // tm, i % (s1 // tm), (i - t1 // tm) % (s2 // tm))

    for layer in range(DEPTH):
        wl = _layer_weights(layer, *weights)
        q_mla, k_mla, v_mla, q_gqa, k_gqa, v_gqa = _proj(x, wl, tables, pos_block, tm)
        groups = ((0, b1, s1), (t1, b2, s2))
        o_mla = jnp.concatenate([
            _attention(q_mla, k_mla, v_mla, row0=r0, n_seq=nb, seq=s, heads=MLA_HEADS, kv_group=1,
                       dq=MLA_QK_PAD, dv=MLA_V, tq=_tile(s, 512), tk=_tile(s, 2048), name="attn_mla")
            for r0, nb, s in groups], axis=0)
        o_gqa = jnp.concatenate([
            _attention(q_gqa, k_gqa, v_gqa, row0=r0, n_seq=nb, seq=s, heads=GQA_HEADS, kv_group=GQA_GROUP,
                       dq=GQA_HD, dv=GQA_HD, tq=_tile(s, 512), tk=_tile(s, 2048), name="attn_gqa")
            for r0, nb, s in groups], axis=0)
        x1, x1p = _outproj(x, o_mla, o_gqa, wl, tm)

        mask, gate, counts = _route(x1, wl['w_rt'], wl['bias'], tm)
        pad_start, fill_start, fill_n, blk_e, n_used = _slot_plan(counts[:, 0], bm, n_blocks)
        slot, w_k = _slots(mask, gate, pad_start.astype(F32).reshape(-1, 1), tm)
        slot_tok = slot.T
        y_in = _dispatch(x1p, slot_tok.reshape(t // tm, 1, tm * TOP_K), fill_start, fill_n, n_blocks * bm, tm)
        y_out = _experts(y_in, blk_e, n_used, w_gate, w_up, w_down, layer, bm)
        final = functools.partial(_final, slot_tok.reshape(t // tm_final, 1, tm_final * TOP_K), y_out, w_k.T, x1,
                                  p[layer], wl, tm_final)
        if layer < DEPTH - 1:
            x = final(0, t // tm_final)
        else:
            y_prompt = final(0, t1 // tm_final)
            y_sample = final(t1 // tm_final, t2 // tm_final)

    return y_prompt.reshape(b1, s1, D_MODEL), y_sample.reshape(b2, s2, D_MODEL)
```

```python
import functools

import jax
import jax.numpy as jnp
from jax import lax
from jax.experimental import pallas as pl
from jax.experimental.pallas import tpu as pltpu

F32 = jnp.float32
BF16 = jnp.bfloat16
U32 = jnp.uint32
I32 = jnp.int32

D_MODEL = 2048
DEPTH = 2
GRID_W = 64
ROPE_THETA = 10000.0
MLA_HEADS = 8
MLA_NOPE = 128
MLA_ROPE = 64
MLA_V = 128
MLA_QK_PAD = 256
Q_LORA = 512
KV_LORA = 512
MLA_WIDTH = MLA_HEADS * MLA_V
MLA_SCALE = (MLA_NOPE + MLA_ROPE) ** -0.5
GQA_HD = 128
GQA_HEADS = 8
GQA_KV_HEADS = 2
GQA_GROUP = GQA_HEADS // GQA_KV_HEADS
GQA_WIDTH = GQA_HEADS * GQA_HD
GQA_SCALE = GQA_HD ** -0.5
N_EXPERTS = 64
TOP_K = 8
N_GROUPS = 8
TOPK_GROUPS = 4
EXPERTS_PER_GROUP = N_EXPERTS // N_GROUPS
D_EXPERT = 512
ROUTED_SCALE = 2.5
PLE_DIM = 256
ALPHA = (2 * DEPTH) ** 0.25
RMS_EPS = 1e-6
LN_EPS = 1e-5
V_EXT = 256
LOG2E = 1.4426950408889634
MAX_STATIC_TICKS = 8
TICKS_PER_TRIP = 2
HALF = D_MODEL // 2

LANES = 128
ROW_TILE = 8
VMEM_LIMIT = 56 * 1024 * 1024

C_CQ = 0
C_CKV = C_CQ + Q_LORA
C_KR = C_CKV + KV_LORA
C_QG = C_KR + LANES
C_KG = C_QG + GQA_WIDTH
C_VG = C_KG + GQA_KV_HEADS * GQA_HD
IN_COLS_PAD = C_VG + GQA_KV_HEADS * GQA_HD


def _tile(n, pref):
    t = min(n, pref)
    while n % t:
        t //= 2
    return t


def _const_spec(shape):
    nd = len(shape)
    return pl.BlockSpec(shape, lambda *_: (0,) * nd, pipeline_mode=pl.Buffered(1))


def _params(sem):
    return pltpu.CompilerParams(dimension_semantics=sem, vmem_limit_bytes=VMEM_LIMIT)


def _pack_pair(lo, hi):
    ulo = lax.bitcast_convert_type(lo.astype(BF16).astype(F32), U32)
    uhi = lax.bitcast_convert_type(hi.astype(BF16).astype(F32), U32)
    return (ulo >> 16) | uhi


def _unpack_pair(u):
    lo = lax.bitcast_convert_type(u << 16, F32)
    hi = lax.bitcast_convert_type(u & jnp.uint32(0xFFFF0000), F32)
    return lo, hi


def _store_row_tiles(ref, packed):
    for j in range(ROW_TILE):
        ref[pl.ds(j, packed.shape[0], stride=ROW_TILE), :] = packed[:, j * LANES:(j + 1) * LANES]


def _load_row_tiles(ref, n):
    return [ref[pl.ds(j, n, stride=ROW_TILE), :] for j in range(ROW_TILE)]


def _rot_half_partner(x):
    lane = lax.broadcasted_iota(I32, x.shape, x.ndim - 1)
    n = x.shape[-1]
    up = pltpu.roll(x, n - 32, x.ndim - 1)
    down = pltpu.roll(x, 32, x.ndim - 1)
    return jnp.where((lane & 32) == 0, up, down)


def _rope(x, cos, sin_signed):
    return x * cos + _rot_half_partner(x) * sin_signed


def _rms(x, g):
    return x * lax.rsqrt(jnp.mean(x * x, axis=-1, keepdims=True) + RMS_EPS) * g


def _layer_norm(x, g, b):
    mu = jnp.mean(x, axis=-1, keepdims=True)
    xc = x - mu
    var = jnp.mean(xc * xc, axis=-1, keepdims=True)
    return xc * lax.rsqrt(var + LN_EPS) * g + b


def _dot(a, b):
    return jnp.dot(a, b, preferred_element_type=F32)


def _proj_kernel(x_ref, w_in_ref, w_uq_ref, w_uk_ref, w_uv_ref, g_q_lat_ref, g_kv_lat_ref,
                 g_q_gqa_ref, g_k_gqa_ref, cos_m_ref, sin_m_ref, cos_a_ref, sin_a_ref,
                 q_mla_ref, k_mla_ref, v_mla_ref, q_gqa_ref, k_gqa_ref, v_gqa_ref):
    xb = x_ref[...].astype(BF16)
    cos_m, sin_m = cos_m_ref[...], sin_m_ref[...]
    cos_a, sin_a = cos_a_ref[...], sin_a_ref[...]

    lat = _dot(xb, w_in_ref[:, C_CQ:C_QG])

    cq = _rms(lat[:, C_CQ:C_CQ + Q_LORA], g_q_lat_ref[...]).astype(BF16)
    q_all = _dot(cq, w_uq_ref[...])
    for h in range(MLA_HEADS):
        c0, c1 = h * MLA_QK_PAD, h * MLA_QK_PAD + MLA_NOPE
        q_rope = _rope(q_all[:, c1:c0 + MLA_QK_PAD], cos_m, sin_m)
        q_mla_ref[:, c0:c1] = (q_all[:, c0:c1] * (MLA_SCALE * LOG2E)).astype(BF16)
        q_mla_ref[:, c1:c0 + MLA_QK_PAD] = (q_rope * (MLA_SCALE * LOG2E)).astype(BF16)

    ckv = _rms(lat[:, C_CKV:C_CKV + KV_LORA], g_kv_lat_ref[...]).astype(BF16)
    k_rope = _rope(lat[:, C_KR:C_KR + LANES], cos_m, sin_m).astype(BF16)
    k_all = _dot(ckv, w_uk_ref[...]).astype(BF16)
    v_all = _dot(ckv, w_uv_ref[...]).astype(BF16)
    ones = jnp.ones((xb.shape[0], V_EXT - MLA_V), BF16)
    for h in range(MLA_HEADS):
        c0 = h * MLA_QK_PAD
        k_mla_ref[:, c0:c0 + MLA_NOPE] = k_all[:, h * MLA_NOPE:(h + 1) * MLA_NOPE]
        k_mla_ref[:, c0 + MLA_NOPE:c0 + MLA_QK_PAD] = k_rope
        v_mla_ref[:, h * V_EXT:h * V_EXT + MLA_V] = v_all[:, h * MLA_V:(h + 1) * MLA_V]
        v_mla_ref[:, h * V_EXT + MLA_V:(h + 1) * V_EXT] = ones

    g_q, g_k = g_q_gqa_ref[...], g_k_gqa_ref[...]
    qg_all = _dot(xb, w_in_ref[:, C_QG:C_KG])
    for h in range(GQA_HEADS):
        qh = _rope(_rms(qg_all[:, h * GQA_HD:(h + 1) * GQA_HD], g_q), cos_a, sin_a) * (GQA_SCALE * LOG2E)
        q_gqa_ref[:, h * GQA_HD:(h + 1) * GQA_HD] = qh.astype(BF16)
    kv_all = _dot(xb, w_in_ref[:, C_KG:IN_COLS_PAD])
    for h in range(GQA_KV_HEADS):
        kh = kv_all[:, h * GQA_HD:(h + 1) * GQA_HD]
        k_gqa_ref[:, h * GQA_HD:(h + 1) * GQA_HD] = _rope(_rms(kh, g_k), cos_a, sin_a).astype(BF16)
        vh = kv_all[:, C_VG - C_KG + h * GQA_HD:C_VG - C_KG + (h + 1) * GQA_HD]
        v_gqa_ref[:, h * V_EXT:h * V_EXT + GQA_HD] = vh.astype(BF16)
        v_gqa_ref[:, h * V_EXT + GQA_HD:(h + 1) * V_EXT] = ones


def _proj(x, wl, tables, pos_block, tm):
    t = x.shape[0]
    row = lambda w: pl.BlockSpec((tm, w), lambda i: (i, 0))
    tab = pl.BlockSpec((tm, LANES), lambda i: (pos_block(i), 0))
    widths = (MLA_HEADS * MLA_QK_PAD, MLA_HEADS * MLA_QK_PAD, MLA_HEADS * V_EXT,
              GQA_WIDTH, GQA_KV_HEADS * GQA_HD, GQA_KV_HEADS * V_EXT)
    consts = (wl['w_in_p'], wl['w_uq_p'], wl['w_uk'], wl['w_uv'],
              wl['g_q_lat'], wl['g_kv_lat'], wl['g_q_gqa'], wl['g_k_gqa'])
    return pl.pallas_call(
        _proj_kernel,
        grid=(t // tm,),
        in_specs=[row(D_MODEL)] + [_const_spec(c.shape) for c in consts] + [tab] * 4,
        out_specs=[row(w) for w in widths],
        out_shape=[jax.ShapeDtypeStruct((t, w), BF16) for w in widths],
        compiler_params=_params(("arbitrary",)),
        name="proj",
    )(x, *consts, *tables)


def _attn_kernel(q_ref, k_ref, v_ref, *rest, tq, tk, dv):
    o_ref, s_ref, p_ref, a_ref, m_ref, acc_ref = rest[-6:]
    nq = q_ref.shape[0] // tq
    nk = k_ref.shape[0] // tk
    n_items = nq * nk

    def split(i):
        return (0, i) if nq == 1 else (i // nk, i % nk)

    def scores(i, slot):
        qt, c = split(i)
        off = pl.multiple_of(c * tk, tk)
        s_ref[slot] = lax.dot_general(q_ref[pl.ds(qt * tq, tq), :], k_ref[pl.ds(off, tk), :],
                                      (((1,), (1,)), ((), ())), preferred_element_type=F32)

    def softmax(i, slot):
        qt, _ = split(i)
        s = s_ref[slot]
        m = m_ref[qt]
        m_new = jnp.maximum(m, jnp.max(s, axis=1, keepdims=True))
        a_ref[slot] = jnp.exp2(m - m_new)
        p_ref[slot] = jnp.exp2(s - m_new).astype(BF16)
        m_ref[qt] = m_new

    def values(i, slot):
        qt, c = split(i)
        off = pl.multiple_of(c * tk, tk)
        acc_ref[qt] = a_ref[slot] * acc_ref[qt] + _dot(p_ref[slot], v_ref[pl.ds(off, tk), :])

    def tick(i, parity, first=False, last=False):
        if not last:
            scores(i + 1, 1 - parity)
        if not first:
            values(i - 1, 1 - parity)
        softmax(i, parity)

    m_ref[...] = jnp.full(m_ref.shape, -jnp.inf, F32)
    acc_ref[...] = jnp.zeros(acc_ref.shape, F32)
    scores(0, 0)
    if n_items <= MAX_STATIC_TICKS:
        for i in range(n_items):
            tick(i, i % 2, first=i == 0, last=i == n_items - 1)
    else:
        assert nq == 1 and nk % 2 == 0
        tick(0, 0, first=True)
        n_loop = (nk - 2) // TICKS_PER_TRIP

        def trip(j, c):
            i = TICKS_PER_TRIP * j + 1
            for d in range(TICKS_PER_TRIP):
                tick(i + d, (1 + d) % 2)
            return c
        lax.fori_loop(0, n_loop, trip, 0)
        for i in range(1 + n_loop * TICKS_PER_TRIP, nk - 1):
            tick(i, i % 2)
        tick(nk - 1, 1, last=True)
    values(n_items - 1, (n_items - 1) % 2)
    for qt in range(nq):
        acc = acc_ref[qt]
        o_ref[qt * tq:(qt + 1) * tq, :] = (acc[:, :dv] / acc[:, dv:]).astype(o_ref.dtype)


def _attention(q, k, v, out, *, row0, n_seq, seq, heads, kv_group, dq, dv, tq, tk, name):
    assert row0 % seq == 0 and seq % tq == 0 and seq % tk == 0 and V_EXT == 2 * dv
    qtiles = max(1, min(seq // tq, MAX_STATIC_TICKS // (seq // tk)))
    bq = qtiles * tq
    assert seq % bq == 0
    nq = seq // bq
    q0, s0 = row0 // bq, row0 // seq
    return pl.pallas_call(
        functools.partial(_attn_kernel, tq=tq, tk=tk, dv=dv),
        grid=(n_seq, heads, nq),
        in_specs=[pl.BlockSpec((bq, dq), lambda b, h, i: (q0 + b * nq + i, h)),
                  pl.BlockSpec((seq, dq), lambda b, h, i: (s0 + b, h // kv_group)),
                  pl.BlockSpec((seq, V_EXT), lambda b, h, i: (s0 + b, h // kv_group))]
        + ([] if out is None else [pl.BlockSpec(memory_space=pl.ANY)]),
        out_specs=pl.BlockSpec((bq, dv), lambda b, h, i: (q0 + b * nq + i, h)),
        out_shape=jax.ShapeDtypeStruct((q.shape[0], heads * dv), BF16),
        input_output_aliases={} if out is None else {3: 0},
        scratch_shapes=[pltpu.VMEM((2, tq, tk), F32), pltpu.VMEM((2, tq, tk), BF16),
                        pltpu.VMEM((2, tq, 1), F32), pltpu.VMEM((qtiles, tq, 1), F32),
                        pltpu.VMEM((qtiles, tq, V_EXT), F32)],
        compiler_params=_params(("arbitrary", "arbitrary", "arbitrary")),
        name=name,
    )(q, k, v, *(() if out is None else (out,)))


def _outproj_kernel(x_ref, om_ref, og_ref, g_om_ref, g_og_ref, w_o_ref, ln_g_ref, ln_b_ref,
                    x1_ref, x1p_ref):
    om = _rms(om_ref[...].astype(F32), g_om_ref[...]).astype(BF16)
    og = _rms(og_ref[...].astype(F32), g_og_ref[...]).astype(BF16)
    mix = _dot(om, w_o_ref[:MLA_WIDTH, :]) + _dot(og, w_o_ref[MLA_WIDTH:, :])
    x1 = _layer_norm(ALPHA * x_ref[...] + mix, ln_g_ref[...], ln_b_ref[...])
    x1_ref[...] = x1
    _store_row_tiles(x1p_ref, _pack_pair(x1[:, :HALF], x1[:, HALF:]))


def _outproj(x, o_mla, o_gqa, wl, tm):
    t = x.shape[0]
    row = lambda w: pl.BlockSpec((tm, w), lambda i: (i, 0))
    consts = (wl['g_o_mla'], wl['g_o_gqa'], wl['w_o'], wl['ln1_g'], wl['ln1_b'])
    return pl.pallas_call(
        _outproj_kernel,
        grid=(t // tm,),
        in_specs=[row(D_MODEL), row(MLA_WIDTH), row(GQA_WIDTH)] + [_const_spec(c.shape) for c in consts],
        out_specs=[row(D_MODEL), pl.BlockSpec((tm * ROW_TILE, LANES), lambda i: (i, 0))],
        out_shape=[jax.ShapeDtypeStruct((t, D_MODEL), F32), jax.ShapeDtypeStruct((t * ROW_TILE, LANES), U32)],
        compiler_params=_params(("arbitrary",)),
        name="outproj",
    )(x, o_mla, o_gqa, *consts)


def _beats(a, ia, b, ib):
    return (a > b) | ((a == b) & (ia < ib))


def _route_kernel(x_ref, w_rt_ref, bias_ref, mask_ref, gate_ref, counts_ref):
    logits = lax.dot_general(w_rt_ref[...], x_ref[...], (((1,), (1,)), ((), ())),
                             precision=lax.Precision.HIGHEST, preferred_element_type=F32)
    scores = jax.nn.sigmoid(logits)
    sel = scores + bias_ref[...]
    tm = sel.shape[1]
    neg = jnp.float32(-jnp.inf)

    sub = lax.broadcasted_iota(I32, (EXPERTS_PER_GROUP, tm), 0)
    grp_rows = []
    for g in range(N_GROUPS):
        blk = sel[g * EXPERTS_PER_GROUP:(g + 1) * EXPERTS_PER_GROUP, :]
        m1 = jnp.max(blk, axis=0, keepdims=True)
        first = jnp.min(jnp.where(blk == m1, sub, EXPERTS_PER_GROUP), axis=0, keepdims=True)
        m2 = jnp.max(jnp.where(sub == first, neg, blk), axis=0, keepdims=True)
        grp_rows.append(m1 + m2)
    grp = jnp.concatenate(grp_rows, axis=0)

    gidx = lax.broadcasted_iota(I32, (N_GROUPS, tm), 0)
    grank = jnp.zeros((N_GROUPS, tm), I32)
    for g in range(N_GROUPS):
        grank += _beats(grp[g:g + 1, :], g, grp, gidx).astype(I32)
    gkeep = grank < TOPK_GROUPS
    keep = jnp.concatenate(
        [jnp.broadcast_to(gkeep[g:g + 1, :], (EXPERTS_PER_GROUP, tm)) for g in range(N_GROUPS)], axis=0)
    sel = jnp.where(keep, sel, neg)

    eidx = lax.broadcasted_iota(I32, (N_EXPERTS, tm), 0).astype(F32)
    mask = jnp.zeros((N_EXPERTS, tm), F32)
    rest = sel
    for _ in range(TOP_K):
        best = jnp.max(rest, axis=0, keepdims=True)
        cand = jnp.where((rest == best) & (mask < 0.5), eidx, float(N_EXPERTS))
        hit = eidx == jnp.min(cand, axis=0, keepdims=True)
        mask = jnp.where(hit, 1.0, mask)
        rest = jnp.where(hit, neg, rest)

    picked = mask * scores
    gate = picked / jnp.sum(picked, axis=0, keepdims=True) * ROUTED_SCALE
    mask_ref[...] = mask
    gate_ref[...] = gate

    @pl.when(pl.program_id(0) == 0)
    def _():
        counts_ref[...] = jnp.zeros_like(counts_ref)
    counts_ref[...] += jnp.broadcast_to(jnp.sum(mask, axis=1, keepdims=True), counts_ref.shape)


def _route(x1, w_rt, bias, tm):
    t = x1.shape[0]
    col = pl.BlockSpec((N_EXPERTS, tm), lambda i: (0, i))
    return pl.pallas_call(
        _route_kernel,
        grid=(t // tm,),
        in_specs=[pl.BlockSpec((tm, D_MODEL), lambda i: (i, 0)),
                  _const_spec(w_rt.shape), _const_spec(bias.shape)],
        out_specs=[col, col, pl.BlockSpec((N_EXPERTS, LANES), lambda i: (0, 0))],
        out_shape=[jax.ShapeDtypeStruct((N_EXPERTS, t), F32), jax.ShapeDtypeStruct((N_EXPERTS, t), F32),
                   jax.ShapeDtypeStruct((N_EXPERTS, LANES), F32)],
        compiler_params=_params(("arbitrary",)),
        name="route",
    )(x1, w_rt, bias)


def _slots_kernel(mask_ref, gate_ref, start_ref, slot_ref, w_ref, seen_ref):
    @pl.when(pl.program_id(0) == 0)
    def _():
        seen_ref[...] = jnp.zeros_like(seen_ref)

    mask = mask_ref[...]
    tm = mask.shape[1]
    mb = mask.astype(BF16)
    before = (lax.broadcasted_iota(I32, (tm, tm), 0) < lax.broadcasted_iota(I32, (tm, tm), 1)).astype(BF16)
    pos = _dot(mb, before)
    lower = (lax.broadcasted_iota(I32, (N_EXPERTS, N_EXPERTS), 1)
             < lax.broadcasted_iota(I32, (N_EXPERTS, N_EXPERTS), 0)).astype(BF16)
    rank = _dot(lower, mb)
    seen = seen_ref[:, 0:1]
    slot = start_ref[...] + seen + pos
    on = mask > 0.5
    gate = gate_ref[...]
    for k in range(TOP_K):
        pick = on & (rank == float(k))
        slot_ref[k:k + 1, :] = jnp.sum(jnp.where(pick, slot, 0.0), axis=0, keepdims=True).astype(I32)
        w_ref[k:k + 1, :] = jnp.sum(jnp.where(pick, gate, 0.0), axis=0, keepdims=True)
    seen_ref[...] += jnp.broadcast_to(jnp.sum(mask, axis=1, keepdims=True), seen_ref.shape)


def _slots(mask, gate, start, tm):
    t = mask.shape[1]
    col = pl.BlockSpec((N_EXPERTS, tm), lambda i: (0, i))
    out = pl.BlockSpec((TOP_K, tm), lambda i: (0, i))
    return pl.pallas_call(
        _slots_kernel,
        grid=(t // tm,),
        in_specs=[col, col, _const_spec(start.shape)],
        out_specs=[out, out],
        out_shape=[jax.ShapeDtypeStruct((TOP_K, t), I32), jax.ShapeDtypeStruct((TOP_K, t), F32)],
        scratch_shapes=[pltpu.VMEM((N_EXPERTS, LANES), F32)],
        compiler_params=_params(("arbitrary",)),
        name="slots",
    )(mask, gate, start)


def _dispatch_kernel(fill_start_ref, fill_n_ref, slot_ref, x_ref, y_hbm, zero_ref, sem, fill_sem, *, tm):
    def issue(t, c):
        for k in range(TOP_K):
            slot = slot_ref[0, 0, t * TOP_K + k]
            pltpu.make_async_copy(x_ref.at[pl.ds(pl.multiple_of(t * ROW_TILE, ROW_TILE), ROW_TILE)],
                                  y_hbm.at[pl.ds(pl.multiple_of(slot * ROW_TILE, ROW_TILE), ROW_TILE)],
                                  sem).start(priority=k % 2)
        return c
    lax.fori_loop(0, tm, issue, 0)

    @pl.when(pl.program_id(0) == 0)
    def _():
        zero_ref[...] = jnp.zeros_like(zero_ref)

        def fill_copy(r):
            return pltpu.make_async_copy(zero_ref, y_hbm.at[pl.ds(pl.multiple_of(r * ROW_TILE, ROW_TILE), ROW_TILE)],
                                         fill_sem)

        def per_expert(e, c):
            lax.fori_loop(0, fill_n_ref[e], lambda j, c2: (fill_copy(fill_start_ref[e] + j).start(), c2)[1], 0)
            lax.fori_loop(0, fill_n_ref[e], lambda j, c2: (fill_copy(fill_start_ref[e] + j).wait(), c2)[1], 0)
            return c
        lax.fori_loop(0, N_EXPERTS, per_expert, 0)

    rows = y_hbm.at[pl.ds(0, TOP_K * tm * ROW_TILE)]
    pltpu.make_async_copy(rows, rows, sem).wait()


def _dispatch(x1p, slot, fill_start, fill_n, n_slots, tm):
    t = x1p.shape[0] // ROW_TILE
    assert TOP_K * tm <= n_slots
    return pl.pallas_call(
        functools.partial(_dispatch_kernel, tm=tm),
        grid_spec=pltpu.PrefetchScalarGridSpec(
            num_scalar_prefetch=2,
            grid=(t // tm,),
            in_specs=[pl.BlockSpec((1, 1, tm * TOP_K), lambda i, *_: (i, 0, 0), memory_space=pltpu.SMEM),
                      pl.BlockSpec((tm * ROW_TILE, LANES), lambda i, *_: (i, 0))],
            out_specs=pl.BlockSpec(memory_space=pl.ANY),
            scratch_shapes=[pltpu.VMEM((ROW_TILE, LANES), U32), pltpu.SemaphoreType.DMA, pltpu.SemaphoreType.DMA]),
        out_shape=jax.ShapeDtypeStruct((n_slots * ROW_TILE, LANES), U32),
        compiler_params=pltpu.CompilerParams(dimension_semantics=("arbitrary",), has_side_effects=True),
        name="dispatch",
    )(fill_start, fill_n, slot, x1p)


def _experts_kernel(blk_e_ref, n_used_ref, y_ref, wg_ref, wu_ref, wd_ref, o_ref, wg_s, wu_s, wd_s):
    b = pl.program_id(0)

    @pl.when(b < n_used_ref[0])
    def _():
        @pl.when((b == 0) | (blk_e_ref[b] != blk_e_ref[jnp.maximum(b - 1, 0)]))
        def _():
            wg_s[...] = wg_ref[0, 0].astype(BF16)
            wu_s[...] = wu_ref[0, 0].astype(BF16)
            wd_s[...] = wd_ref[0, 0].astype(BF16)

        bm = y_ref.shape[0] // ROW_TILE
        halves = [_unpack_pair(u) for u in _load_row_tiles(y_ref, bm)]
        x = jnp.concatenate([lo.astype(BF16) for lo, _ in halves] + [hi.astype(BF16) for _, hi in halves], axis=1)
        hid = (jax.nn.silu(_dot(x, wg_s[...])) * _dot(x, wu_s[...])).astype(BF16)
        out = _dot(hid, wd_s[...])
        _store_row_tiles(o_ref, _pack_pair(out[:, :HALF], out[:, HALF:]))


def _experts(y_in, blk_e, n_used, w_gate, w_up, w_down, layer, bm):
    n_slots = y_in.shape[0] // ROW_TILE
    blk = lambda b, be, nu: (jnp.minimum(b, nu[0] - 1), 0)
    wsel = lambda b, be, nu: (layer, be[b], 0, 0)
    return pl.pallas_call(
        _experts_kernel,
        grid_spec=pltpu.PrefetchScalarGridSpec(
            num_scalar_prefetch=2,
            grid=(n_slots // bm,),
            in_specs=[pl.BlockSpec((bm * ROW_TILE, LANES), blk),
                      pl.BlockSpec((1, 1, D_MODEL, D_EXPERT), wsel),
                      pl.BlockSpec((1, 1, D_MODEL, D_EXPERT), wsel),
                      pl.BlockSpec((1, 1, D_EXPERT, D_MODEL), wsel)],
            out_specs=pl.BlockSpec((bm * ROW_TILE, LANES), blk),
            scratch_shapes=[pltpu.VMEM((D_MODEL, D_EXPERT), BF16), pltpu.VMEM((D_MODEL, D_EXPERT), BF16),
                            pltpu.VMEM((D_EXPERT, D_MODEL), BF16)]),
        out_shape=jax.ShapeDtypeStruct((n_slots * ROW_TILE, LANES), U32),
        compiler_params=_params(("arbitrary",)),
        name="experts",
    )(blk_e, n_used, y_in, w_gate, w_up, w_down)


def _final_kernel(slot_ref, slot_next_ref, y_hbm, w_ref, x1_ref, p_ref, ws_gate_ref, ws_up_ref, ws_down_ref,
                  w_pg_ref, w_ple_ref, ln_g_ref, ln_b_ref, o_ref, rows_ref, sem, *, tm):
    i = pl.program_id(0)
    cur = i % 2

    def gather(ids_ref, buf):
        def issue(pair, c):
            for t in (2 * pair, 2 * pair + 1):
                dst_row = pl.multiple_of(t * ROW_TILE, ROW_TILE)
                for k in range(TOP_K):
                    src_row = pl.multiple_of(ids_ref[0, 0, t * TOP_K + k] * ROW_TILE, ROW_TILE)
                    pltpu.make_async_copy(y_hbm.at[pl.ds(src_row, ROW_TILE)],
                                          rows_ref.at[buf, k, pl.ds(dst_row, ROW_TILE)], sem.at[buf]).start()
            return c
        lax.fori_loop(0, tm // 2, issue, 0)

    @pl.when(i == 0)
    def _():
        gather(slot_ref, 0)

    has_next = i + 1 < pl.num_programs(0)
    for buf in range(2):
        @pl.when(has_next & (cur != buf))
        def _():
            gather(slot_next_ref, buf)

    x1 = x1_ref[...]
    xb = x1.astype(BF16)
    hid = (jax.nn.silu(_dot(xb, ws_gate_ref[...])) * _dot(xb, ws_up_ref[...])).astype(BF16)
    shared = _dot(hid, ws_down_ref[...])
    ple = jax.nn.sigmoid(_dot(xb, w_pg_ref[...])) * _dot(p_ref[...].astype(BF16), w_ple_ref[...])
    base = ALPHA * x1 + shared + ple

    pltpu.make_async_copy(rows_ref.at[cur], rows_ref.at[cur], sem.at[cur]).wait()

    w = w_ref[...]
    wk = [jnp.broadcast_to(w[:, k:k + 1], (tm, LANES)) for k in range(TOP_K)]
    r_lo, r_hi = [], []
    for j in range(ROW_TILE):
        a_lo = jnp.zeros((tm, LANES), F32)
        a_hi = jnp.zeros((tm, LANES), F32)
        for k in range(TOP_K):
            lo, hi = _unpack_pair(rows_ref[cur, k, pl.ds(j, tm, stride=ROW_TILE), :])
            a_lo += wk[k] * lo
            a_hi += wk[k] * hi
        r_lo.append(a_lo)
        r_hi.append(a_hi)
    y = base + jnp.concatenate(r_lo + r_hi, axis=1)
    o_ref[...] = _layer_norm(y, ln_g_ref[...], ln_b_ref[...])


def _final(slot, y_out, w_tok, x1, p, wl, tm, tile0, n):
    row = lambda w: pl.BlockSpec((tm, w), lambda i: (tile0 + i, 0))
    ids = lambda index: pl.BlockSpec((1, 1, tm * TOP_K), index, memory_space=pltpu.SMEM)
    consts = (wl['ws_gate'], wl['ws_up'], wl['ws_down'], wl['w_ple_gate'], wl['w_ple'], wl['ln2_g'], wl['ln2_b'])
    return pl.pallas_call(
        functools.partial(_final_kernel, tm=tm),
        grid=(n,),
        in_specs=[ids(lambda i: (tile0 + i, 0, 0)), ids(lambda i: (tile0 + jnp.minimum(i + 1, n - 1), 0, 0)),
                  pl.BlockSpec(memory_space=pl.ANY),
                  row(TOP_K), row(D_MODEL), row(PLE_DIM)] + [_const_spec(c.shape) for c in consts],
        out_specs=pl.BlockSpec((tm, D_MODEL), lambda i: (i, 0)),
        out_shape=jax.ShapeDtypeStruct((n * tm, D_MODEL), F32),
        scratch_shapes=[pltpu.VMEM((2, TOP_K, tm * ROW_TILE, LANES), U32), pltpu.SemaphoreType.DMA((2,))],
        compiler_params=_params(("arbitrary",)),
        name="final",
    )(slot, slot, y_out, w_tok, x1, p, *consts)


def _rope_tables(n):
    t = jnp.arange(n, dtype=F32)
    inv = ROPE_THETA ** (-jnp.arange(0, MLA_ROPE, 2, dtype=F32) / MLA_ROPE)
    a_seq = t[:, None] * inv[None, :]
    a_row = jnp.floor(t / GRID_W)[:, None] * inv[None, :]
    a_col = (t - jnp.floor(t / GRID_W) * GRID_W)[:, None] * inv[None, :]
    z = jnp.zeros((n, LANES // 2), F32)
    cos_m = jnp.concatenate([jnp.cos(a_seq), jnp.cos(a_seq), z], axis=1)
    sin_m = jnp.concatenate([-jnp.sin(a_seq), jnp.sin(a_seq), z], axis=1)
    cos_a = jnp.concatenate([jnp.cos(a_row), jnp.cos(a_row), jnp.cos(a_col), jnp.cos(a_col)], axis=1)
    sin_a = jnp.concatenate([-jnp.sin(a_row), jnp.sin(a_row), -jnp.sin(a_col), jnp.sin(a_col)], axis=1)
    return cos_m, sin_m, cos_a, sin_a


def _layer_weights(i, w_in, g_q_lat, g_kv_lat, w_uq, w_uk, w_uv, g_q_gqa, g_k_gqa, g_o_mla, g_o_gqa,
                   w_o, ln1_g, ln1_b, w_router, router_bias, w_gate, w_up, w_down, ws_gate, ws_up,
                   ws_down, w_ple_gate, w_ple, ln2_g, ln2_b):
    wi = w_in[i]
    kr0 = Q_LORA + KV_LORA
    w_in_p = jnp.concatenate(
        [wi[:, :kr0 + MLA_ROPE], jnp.zeros((D_MODEL, LANES - MLA_ROPE), F32), wi[:, kr0 + MLA_ROPE:]], axis=1)
    uq = w_uq[i].reshape(Q_LORA, MLA_HEADS, MLA_NOPE + MLA_ROPE)
    uq = jnp.pad(uq, ((0, 0), (0, 0), (0, MLA_QK_PAD - MLA_NOPE - MLA_ROPE)))
    vec = lambda a: a[i].reshape(1, -1).astype(F32)
    return dict(
        w_in_p=w_in_p.astype(BF16), w_uq_p=uq.reshape(Q_LORA, MLA_HEADS * MLA_QK_PAD).astype(BF16),
        w_uk=w_uk[i].astype(BF16), w_uv=w_uv[i].astype(BF16),
        g_q_lat=vec(g_q_lat), g_kv_lat=vec(g_kv_lat), g_q_gqa=vec(g_q_gqa), g_k_gqa=vec(g_k_gqa),
        g_o_mla=vec(g_o_mla), g_o_gqa=vec(g_o_gqa), w_o=w_o[i].astype(BF16),
        ln1_g=vec(ln1_g), ln1_b=vec(ln1_b),
        w_rt=w_router[i].T.astype(F32), bias=router_bias[i].reshape(-1, 1).astype(F32),
        ws_gate=ws_gate[i].astype(BF16), ws_up=ws_up[i].astype(BF16), ws_down=ws_down[i].astype(BF16),
        w_ple_gate=w_ple_gate[i].astype(BF16), w_ple=w_ple[i].astype(BF16),
        ln2_g=vec(ln2_g), ln2_b=vec(ln2_b))


def _slot_plan(counts, bm, n_blocks):
    counts = counts.astype(I32)
    padded = (counts + bm - 1) // bm * bm
    pad_end = jnp.cumsum(padded)
    pad_start = pad_end - padded
    first_row = jnp.arange(n_blocks, dtype=I32) * bm
    blk_e = jnp.minimum(jnp.sum(pad_end[None, :] <= first_row[:, None], axis=1), N_EXPERTS - 1).astype(I32)
    n_used = jnp.maximum(pad_end[-1:] // bm, 1).astype(I32)
    return pad_start, pad_start + counts, padded - counts, blk_e, n_used


def kernel(x_prompt, x_sample, p_prompt, p_sample, w_in, g_q_lat, g_kv_lat, w_uq, w_uk, w_uv,
           g_q_gqa, g_k_gqa, g_o_mla, g_o_gqa, w_o, ln1_g, ln1_b, w_router, router_bias,
           w_gate, w_up, w_down, ws_gate, ws_up, ws_down, w_ple_gate, w_ple, ln2_g, ln2_b):
    weights = (w_in, g_q_lat, g_kv_lat, w_uq, w_uk, w_uv, g_q_gqa, g_k_gqa, g_o_mla, g_o_gqa, w_o,
               ln1_g, ln1_b, w_router, router_bias, w_gate, w_up, w_down, ws_gate, ws_up, ws_down,
               w_ple_gate, w_ple, ln2_g, ln2_b)
    b1, s1, _ = x_prompt.shape
    b2, s2, _ = x_sample.shape
    t1, t2 = b1 * s1, b2 * s2
    t = t1 + t2
    x = jnp.concatenate([x_prompt.reshape(t1, D_MODEL), x_sample.reshape(t2, D_MODEL)], axis=0)
    p = jnp.concatenate([p_prompt.reshape(DEPTH, t1, PLE_DIM), p_sample.reshape(DEPTH, t2, PLE_DIM)], axis=1)

    tm = _tile(min(s1, s2), 512)
    tm_final = _tile(min(s1, s2), 256)
    bm =_tile(t * TOP_K // N_EXPERTS, 512)
    n_blocks = t * TOP_K // bm + N_EXPERTS
    tables = _rope_tables(max(s1, s2))

    def pos_block(i):
        return jnp.where(i < t1 // tm, i % (s1 // tm), (i - t1 // tm) % (s2 // tm))

    for layer in range(DEPTH):
        wl = _layer_weights(layer, *weights)
        q_mla, k_mla, v_mla, q_gqa, k_gqa, v_gqa = _proj(x, wl, tables, pos_block, tm)
        groups = ((0, b1, s1), (t1, b2, s2))
        o_mla = o_gqa = None
        for r0, nb, s in groups:
            o_mla = _attention(q_mla, k_mla, v_mla, o_mla, row0=r0, n_seq=nb, seq=s, heads=MLA_HEADS, kv_group=1,
                               dq=MLA_QK_PAD, dv=MLA_V, tq=_tile(s, 512), tk=_tile(s, 2048), name="attn_mla")
            o_gqa = _attention(q_gqa, k_gqa, v_gqa, o_gqa, row0=r0, n_seq=nb, seq=s, heads=GQA_HEADS,
                               kv_group=GQA_GROUP, dq=GQA_HD, dv=GQA_HD, tq=_tile(s, 512), tk=_tile(s, 2048),
                               name="attn_gqa")
        x1, x1p = _outproj(x, o_mla, o_gqa, wl, tm)

        mask, gate, counts = _route(x1, wl['w_rt'], wl['bias'], tm)
        pad_start, fill_start, fill_n, blk_e, n_used = _slot_plan(counts[:, 0], bm, n_blocks)
        slot, w_k = _slots(mask, gate, pad_start.astype(F32).reshape(-1, 1), tm)
        slot_tok = slot.T
        y_in = _dispatch(x1p, slot_tok.reshape(t // tm, 1, tm * TOP_K), fill_start, fill_n, n_blocks * bm, tm)
        y_out = _experts(y_in, blk_e, n_used, w_gate, w_up, w_down, layer, bm)
        final = functools.partial(_final, slot_tok.reshape(t // tm_final, 1, tm_final * TOP_K), y_out, w_k.T, x1,
                                  p[layer], wl, tm_final)
        if layer < DEPTH - 1:
            x = final(0, t // tm_final)
        else:
            y_prompt = final(0, t1 // tm_final)
            y_sample = final(t1 // tm_final, t2 // tm_final)

    return y_prompt.reshape(b1, s1, D_MODEL), y_sample.reshape(b2, s2, D_MODEL)
```

```python
import functools

import jax
import jax.numpy as jnp
from jax import lax
from jax.experimental import pallas as pl
from jax.experimental.pallas import tpu as pltpu

F32 = jnp.float32
BF16 = jnp.bfloat16
U32 = jnp.uint32
I32 = jnp.int32

D_MODEL = 2048
DEPTH = 2
GRID_W = 64
ROPE_THETA = 10000.0
MLA_HEADS = 8
MLA_NOPE = 128
MLA_ROPE = 64
MLA_V = 128
MLA_QK_PAD = 256
Q_LORA = 512
KV_LORA = 512
MLA_WIDTH = MLA_HEADS * MLA_V
MLA_SCALE = (MLA_NOPE + MLA_ROPE) ** -0.5
GQA_HD = 128
GQA_HEADS = 8
GQA_KV_HEADS = 2
GQA_GROUP = GQA_HEADS // GQA_KV_HEADS
GQA_WIDTH = GQA_HEADS * GQA_HD
GQA_SCALE = GQA_HD ** -0.5
N_EXPERTS = 64
TOP_K = 8
N_GROUPS = 8
TOPK_GROUPS = 4
EXPERTS_PER_GROUP = N_EXPERTS // N_GROUPS
D_EXPERT = 512
ROUTED_SCALE = 2.5
PLE_DIM = 256
ALPHA = (2 * DEPTH) ** 0.25
RMS_EPS = 1e-6
LN_EPS = 1e-5
V_EXT = 256
LOG2E = 1.4426950408889634
MAX_STATIC_TICKS = 8
TICKS_PER_TRIP = 2
HALF = D_MODEL // 2

LANES = 128
ROW_TILE = 8
VMEM_LIMIT = 56 * 1024 * 1024

C_CQ = 0
C_CKV = C_CQ + Q_LORA
C_KR = C_CKV + KV_LORA
C_QG = C_KR + LANES
C_KG = C_QG + GQA_WIDTH
C_VG = C_KG + GQA_KV_HEADS * GQA_HD
IN_COLS_PAD = C_VG + GQA_KV_HEADS * GQA_HD


def _tile(n, pref):
    t = min(n, pref)
    while n % t:
        t //= 2
    return t


def _const_spec(shape):
    nd = len(shape)
    return pl.BlockSpec(shape, lambda *_: (0,) * nd, pipeline_mode=pl.Buffered(1))


def _params(sem):
    return pltpu.CompilerParams(dimension_semantics=sem, vmem_limit_bytes=VMEM_LIMIT)


def _pack_pair(lo, hi):
    ulo = lax.bitcast_convert_type(lo.astype(BF16).astype(F32), U32)
    uhi = lax.bitcast_convert_type(hi.astype(BF16).astype(F32), U32)
    return (ulo >> 16) | uhi


def _unpack_pair(u):
    lo = lax.bitcast_convert_type(u << 16, F32)
    hi = lax.bitcast_convert_type(u & jnp.uint32(0xFFFF0000), F32)
    return lo, hi


def _store_row_tiles(ref, packed):
    for j in range(ROW_TILE):
        ref[pl.ds(j, packed.shape[0], stride=ROW_TILE), :] = packed[:, j * LANES:(j + 1) * LANES]


def _load_row_tiles(ref, n):
    return [ref[pl.ds(j, n, stride=ROW_TILE), :] for j in range(ROW_TILE)]


def _rot_half_partner(x):
    lane = lax.broadcasted_iota(I32, x.shape, x.ndim - 1)
    n = x.shape[-1]
    up = pltpu.roll(x, n - 32, x.ndim - 1)
    down = pltpu.roll(x, 32, x.ndim - 1)
    return jnp.where((lane & 32) == 0, up, down)


def _rope(x, cos, sin_signed):
    return x * cos + _rot_half_partner(x) * sin_signed


def _rms(x, g):
    return x * lax.rsqrt(jnp.mean(x * x, axis=-1, keepdims=True) + RMS_EPS) * g


def _layer_norm(x, g, b):
    mu = jnp.mean(x, axis=-1, keepdims=True)
    xc = x - mu
    var = jnp.mean(xc * xc, axis=-1, keepdims=True)
    return xc * lax.rsqrt(var + LN_EPS) * g + b


def _dot(a, b):
    return jnp.dot(a, b, preferred_element_type=F32)


def _proj_kernel(x_ref, w_in_ref, w_uq_ref, w_uk_ref, w_uv_ref, g_q_lat_ref, g_kv_lat_ref,
                 g_q_gqa_ref, g_k_gqa_ref, cos_m_ref, sin_m_ref, cos_a_ref, sin_a_ref,
                 q_mla_ref, k_mla_ref, v_mla_ref, q_gqa_ref, k_gqa_ref, v_gqa_ref):
    xb = x_ref[...].astype(BF16)
    cos_m, sin_m = cos_m_ref[...], sin_m_ref[...]
    cos_a, sin_a = cos_a_ref[...], sin_a_ref[...]

    lat = _dot(xb, w_in_ref[:, C_CQ:C_QG])

    cq = _rms(lat[:, C_CQ:C_CQ + Q_LORA], g_q_lat_ref[...]).astype(BF16)
    q_all = _dot(cq, w_uq_ref[...])
    for h in range(MLA_HEADS):
        c0, c1 = h * MLA_QK_PAD, h * MLA_QK_PAD + MLA_NOPE
        q_rope = _rope(q_all[:, c1:c0 + MLA_QK_PAD], cos_m, sin_m)
        q_mla_ref[:, c0:c1] = (q_all[:, c0:c1] * (MLA_SCALE * LOG2E)).astype(BF16)
        q_mla_ref[:, c1:c0 + MLA_QK_PAD] = (q_rope * (MLA_SCALE * LOG2E)).astype(BF16)

    ckv = _rms(lat[:, C_CKV:C_CKV + KV_LORA], g_kv_lat_ref[...]).astype(BF16)
    k_rope = _rope(lat[:, C_KR:C_KR + LANES], cos_m, sin_m).astype(BF16)
    k_all = _dot(ckv, w_uk_ref[...]).astype(BF16)
    v_all = _dot(ckv, w_uv_ref[...]).astype(BF16)
    ones = jnp.ones((xb.shape[0], V_EXT - MLA_V), BF16)
    for h in range(MLA_HEADS):
        c0 = h * MLA_QK_PAD
        k_mla_ref[:, c0:c0 + MLA_NOPE] = k_all[:, h * MLA_NOPE:(h + 1) * MLA_NOPE]
        k_mla_ref[:, c0 + MLA_NOPE:c0 + MLA_QK_PAD] = k_rope
        v_mla_ref[:, h * V_EXT:h * V_EXT + MLA_V] = v_all[:, h * MLA_V:(h + 1) * MLA_V]
        v_mla_ref[:, h * V_EXT + MLA_V:(h + 1) * V_EXT] = ones

    g_q, g_k = g_q_gqa_ref[...], g_k_gqa_ref[...]
    qg_all = _dot(xb, w_in_ref[:, C_QG:C_KG])
    for h in range(GQA_HEADS):
        qh = _rope(_rms(qg_all[:, h * GQA_HD:(h + 1) * GQA_HD], g_q), cos_a, sin_a) * (GQA_SCALE * LOG2E)
        q_gqa_ref[:, h * GQA_HD:(h + 1) * GQA_HD] = qh.astype(BF16)
    kv_all = _dot(xb, w_in_ref[:, C_KG:IN_COLS_PAD])
    for h in range(GQA_KV_HEADS):
        kh = kv_all[:, h * GQA_HD:(h + 1) * GQA_HD]
        k_gqa_ref[:, h * GQA_HD:(h + 1) * GQA_HD] = _rope(_rms(kh, g_k), cos_a, sin_a).astype(BF16)
        vh = kv_all[:, C_VG - C_KG + h * GQA_HD:C_VG - C_KG + (h + 1) * GQA_HD]
        v_gqa_ref[:, h * V_EXT:h * V_EXT + GQA_HD] = vh.astype(BF16)
        v_gqa_ref[:, h * V_EXT + GQA_HD:(h + 1) * V_EXT] = ones


def _proj(x, wl, tables, pos_block, tm):
    t = x.shape[0]
    row = lambda w: pl.BlockSpec((tm, w), lambda i: (i, 0))
    tab = pl.BlockSpec((tm, LANES), lambda i: (pos_block(i), 0))
    widths = (MLA_HEADS * MLA_QK_PAD, MLA_HEADS * MLA_QK_PAD, MLA_HEADS * V_EXT,
              GQA_WIDTH, GQA_KV_HEADS * GQA_HD, GQA_KV_HEADS * V_EXT)
    consts = (wl['w_in_p'], wl['w_uq_p'], wl['w_uk'], wl['w_uv'],
              wl['g_q_lat'], wl['g_kv_lat'], wl['g_q_gqa'], wl['g_k_gqa'])
    return pl.pallas_call(
        _proj_kernel,
        grid=(t // tm,),
        in_specs=[row(D_MODEL)] + [_const_spec(c.shape) for c in consts] + [tab] * 4,
        out_specs=[row(w) for w in widths],
        out_shape=[jax.ShapeDtypeStruct((t, w), BF16) for w in widths],
        compiler_params=_params(("arbitrary",)),
        name="proj",
    )(x, *consts, *tables)


def _attn_kernel(q_ref, k_ref, v_ref, *rest, tq, tk, dv):
    o_ref, s_ref, p_ref, a_ref, m_ref, acc_ref = rest[-6:]
    nq = q_ref.shape[0] // tq
    nk = k_ref.shape[0] // tk
    n_items = nq * nk

    def split(i):
        return (0, i) if nq == 1 else (i // nk, i % nk)

    def scores(i, slot):
        qt, c = split(i)
        off = pl.multiple_of(c * tk, tk)
        s_ref[slot] = lax.dot_general(q_ref[pl.ds(qt * tq, tq), :], k_ref[pl.ds(off, tk), :],
                                      (((1,), (1,)), ((), ())), preferred_element_type=F32)

    def softmax(i, slot):
        qt, _ = split(i)
        s = s_ref[slot]
        m = m_ref[qt]
        m_new = jnp.maximum(m, jnp.max(s, axis=1, keepdims=True))
        a_ref[slot] = jnp.exp2(m - m_new)
        p_ref[slot] = jnp.exp2(s - m_new).astype(BF16)
        m_ref[qt] = m_new

    def values(i, slot):
        qt, c = split(i)
        off = pl.multiple_of(c * tk, tk)
        acc_ref[qt] = a_ref[slot] * acc_ref[qt] + _dot(p_ref[slot], v_ref[pl.ds(off, tk), :])

    def tick(i, parity, first=False, last=False):
        if not last:
            scores(i + 1, 1 - parity)
        if not first:
            values(i - 1, 1 - parity)
        softmax(i, parity)

    m_ref[...] = jnp.full(m_ref.shape, -jnp.inf, F32)
    acc_ref[...] = jnp.zeros(acc_ref.shape, F32)
    scores(0, 0)
    if n_items <= MAX_STATIC_TICKS:
        for i in range(n_items):
            tick(i, i % 2, first=i == 0, last=i == n_items - 1)
    else:
        assert nq == 1 and nk % 2 == 0
        tick(0, 0, first=True)
        n_loop = (nk - 2) // TICKS_PER_TRIP

        def trip(j, c):
            i = TICKS_PER_TRIP * j + 1
            for d in range(TICKS_PER_TRIP):
                tick(i + d, (1 + d) % 2)
            return c
        lax.fori_loop(0, n_loop, trip, 0)
        for i in range(1 + n_loop * TICKS_PER_TRIP, nk - 1):
            tick(i, i % 2)
        tick(nk - 1, 1, last=True)
    values(n_items - 1, (n_items - 1) % 2)
    for qt in range(nq):
        acc = acc_ref[qt]
        o_ref[qt * tq:(qt + 1) * tq, :] = (acc[:, :dv] / acc[:, dv:]).astype(o_ref.dtype)


def _attention(q, k, v, out, *, row0, n_seq, seq, heads, kv_group, dq, dv, tq, tk, name):
    assert row0 % seq == 0 and seq % tq == 0 and seq % tk == 0 and V_EXT == 2 * dv
    qtiles = max(1, min(seq // tq, MAX_STATIC_TICKS // (seq // tk)))
    bq = qtiles * tq
    assert seq % bq == 0
    nq = seq // bq
    q0, s0 = row0 // bq, row0 // seq
    return pl.pallas_call(
        functools.partial(_attn_kernel, tq=tq, tk=tk, dv=dv),
        grid=(n_seq, heads, nq),
        in_specs=[pl.BlockSpec((bq, dq), lambda b, h, i: (q0 + b * nq + i, h)),
                  pl.BlockSpec((seq, dq), lambda b, h, i: (s0 + b, h // kv_group)),
                  pl.BlockSpec((seq, V_EXT), lambda b, h, i: (s0 + b, h // kv_group))]
        + ([] if out is None else [pl.BlockSpec(memory_space=pl.ANY)]),
        out_specs=pl.BlockSpec((bq, dv), lambda b, h, i: (q0 + b * nq + i, h)),
        out_shape=jax.ShapeDtypeStruct((q.shape[0], heads * dv), BF16),
        input_output_aliases={} if out is None else {3: 0},
        scratch_shapes=[pltpu.VMEM((2, tq, tk), F32), pltpu.VMEM((2, tq, tk), BF16),
                        pltpu.VMEM((2, tq, 1), F32), pltpu.VMEM((qtiles, tq, 1), F32),
                        pltpu.VMEM((qtiles, tq, V_EXT), F32)],
        compiler_params=_params(("arbitrary", "arbitrary", "arbitrary")),
        name=name,
    )(q, k, v, *(() if out is None else (out,)))


def _outproj_kernel(x_ref, om_ref, og_ref, g_om_ref, g_og_ref, w_o_ref, ln_g_ref, ln_b_ref,
                    x1_ref, x1p_ref):
    om = _rms(om_ref[...].astype(F32), g_om_ref[...]).astype(BF16)
    og = _rms(og_ref[...].astype(F32), g_og_ref[...]).astype(BF16)
    mix = _dot(om, w_o_ref[:MLA_WIDTH, :]) + _dot(og, w_o_ref[MLA_WIDTH:, :])
    x1 = _layer_norm(ALPHA * x_ref[...] + mix, ln_g_ref[...], ln_b_ref[...])
    x1_ref[...] = x1
    _store_row_tiles(x1p_ref, _pack_pair(x1[:, :HALF], x1[:, HALF:]))


def _outproj(x, o_mla, o_gqa, wl, tm):
    t = x.shape[0]
    row = lambda w: pl.BlockSpec((tm, w), lambda i: (i, 0))
    consts = (wl['g_o_mla'], wl['g_o_gqa'], wl['w_o'], wl['ln1_g'], wl['ln1_b'])
    return pl.pallas_call(
        _outproj_kernel,
        grid=(t // tm,),
        in_specs=[row(D_MODEL), row(MLA_WIDTH), row(GQA_WIDTH)] + [_const_spec(c.shape) for c in consts],
        out_specs=[row(D_MODEL), pl.BlockSpec((tm * ROW_TILE, LANES), lambda i: (i, 0))],
        out_shape=[jax.ShapeDtypeStruct((t, D_MODEL), F32), jax.ShapeDtypeStruct((t * ROW_TILE, LANES), U32)],
        compiler_params=_params(("arbitrary",)),
        name="outproj",
    )(x, o_mla, o_gqa, *consts)


def _beats(a, ia, b, ib):
    return (a > b) | ((a == b) & (ia < ib))


def _route_kernel(x_ref, w_rt_ref, bias_ref, mask_ref, gate_ref, counts_ref):
    logits = lax.dot_general(w_rt_ref[...], x_ref[...], (((1,), (1,)), ((), ())),
                             precision=lax.Precision.HIGHEST, preferred_element_type=F32)
    scores = jax.nn.sigmoid(logits)
    sel = scores + bias_ref[...]
    tm = sel.shape[1]
    neg = jnp.float32(-jnp.inf)

    sub = lax.broadcasted_iota(I32, (EXPERTS_PER_GROUP, tm), 0)
    grp_rows = []
    for g in range(N_GROUPS):
        blk = sel[g * EXPERTS_PER_GROUP:(g + 1) * EXPERTS_PER_GROUP, :]
        m1 = jnp.max(blk, axis=0, keepdims=True)
        first = jnp.min(jnp.where(blk == m1, sub, EXPERTS_PER_GROUP), axis=0, keepdims=True)
        m2 = jnp.max(jnp.where(sub == first, neg, blk), axis=0, keepdims=True)
        grp_rows.append(m1 + m2)
    grp = jnp.concatenate(grp_rows, axis=0)

    gidx = lax.broadcasted_iota(I32, (N_GROUPS, tm), 0)
    grank = jnp.zeros((N_GROUPS, tm), I32)
    for g in range(N_GROUPS):
        grank += _beats(grp[g:g + 1, :], g, grp, gidx).astype(I32)
    gkeep = grank < TOPK_GROUPS
    keep = jnp.concatenate(
        [jnp.broadcast_to(gkeep[g:g + 1, :], (EXPERTS_PER_GROUP, tm)) for g in range(N_GROUPS)], axis=0)
    sel = jnp.where(keep, sel, neg)

    eidx = lax.broadcasted_iota(I32, (N_EXPERTS, tm), 0).astype(F32)
    mask = jnp.zeros((N_EXPERTS, tm), F32)
    rest = sel
    for _ in range(TOP_K):
        best = jnp.max(rest, axis=0, keepdims=True)
        cand = jnp.where((rest == best) & (mask < 0.5), eidx, float(N_EXPERTS))
        hit = eidx == jnp.min(cand, axis=0, keepdims=True)
        mask = jnp.where(hit, 1.0, mask)
        rest = jnp.where(hit, neg, rest)

    picked = mask * scores
    gate = picked / jnp.sum(picked, axis=0, keepdims=True) * ROUTED_SCALE
    mask_ref[...] = mask
    gate_ref[...] = gate

    @pl.when(pl.program_id(0) == 0)
    def _():
        counts_ref[...] = jnp.zeros_like(counts_ref)
    counts_ref[...] += jnp.broadcast_to(jnp.sum(mask, axis=1, keepdims=True), counts_ref.shape)


def _route(x1, w_rt, bias, tm):
    t = x1.shape[0]
    col = pl.BlockSpec((N_EXPERTS, tm), lambda i: (0, i))
    return pl.pallas_call(
        _route_kernel,
        grid=(t // tm,),
        in_specs=[pl.BlockSpec((tm, D_MODEL), lambda i: (i, 0)),
                  _const_spec(w_rt.shape), _const_spec(bias.shape)],
        out_specs=[col, col, pl.BlockSpec((N_EXPERTS, LANES), lambda i: (0, 0))],
        out_shape=[jax.ShapeDtypeStruct((N_EXPERTS, t), F32), jax.ShapeDtypeStruct((N_EXPERTS, t), F32),
                   jax.ShapeDtypeStruct((N_EXPERTS, LANES), F32)],
        compiler_params=_params(("arbitrary",)),
        name="route",
    )(x1, w_rt, bias)


def _slots_kernel(mask_ref, gate_ref, start_ref, slot_ref, w_ref, seen_ref):
    @pl.when(pl.program_id(0) == 0)
    def _():
        seen_ref[...] = jnp.zeros_like(seen_ref)

    mask = mask_ref[...]
    tm = mask.shape[1]
    mb = mask.astype(BF16)
    before = (lax.broadcasted_iota(I32, (tm, tm), 0) < lax.broadcasted_iota(I32, (tm, tm), 1)).astype(BF16)
    pos = _dot(mb, before)
    lower = (lax.broadcasted_iota(I32, (N_EXPERTS, N_EXPERTS), 1)
             < lax.broadcasted_iota(I32, (N_EXPERTS, N_EXPERTS), 0)).astype(BF16)
    rank = _dot(lower, mb)
    seen = seen_ref[:, 0:1]
    slot = start_ref[...] + seen + pos
    on = mask > 0.5
    gate = gate_ref[...]
    for k in range(TOP_K):
        pick = on & (rank == float(k))
        slot_ref[k:k + 1, :] = jnp.sum(jnp.where(pick, slot, 0.0), axis=0, keepdims=True).astype(I32)
        w_ref[k:k + 1, :] = jnp.sum(jnp.where(pick, gate, 0.0), axis=0, keepdims=True)
    seen_ref[...] += jnp.broadcast_to(jnp.sum(mask, axis=1, keepdims=True), seen_ref.shape)


def _slots(mask, gate, start, tm):
    t = mask.shape[1]
    col = pl.BlockSpec((N_EXPERTS, tm), lambda i: (0, i))
    out = pl.BlockSpec((TOP_K, tm), lambda i: (0, i))
    return pl.pallas_call(
        _slots_kernel,
        grid=(t // tm,),
        in_specs=[col, col, _const_spec(start.shape)],
        out_specs=[out, out],
        out_shape=[jax.ShapeDtypeStruct((TOP_K, t), I32), jax.ShapeDtypeStruct((TOP_K, t), F32)],
        scratch_shapes=[pltpu.VMEM((N_EXPERTS, LANES), F32)],
        compiler_params=_params(("arbitrary",)),
        name="slots",
    )(mask, gate, start)


def _dispatch_kernel(fill_start_ref, fill_n_ref, slot_ref, x_ref, x1_ref, p_ref, ws_gate_ref, ws_up_ref,
                     ws_down_ref, w_pg_ref, w_ple_ref, y_hbm, base_ref, zero_ref, sem, fill_sem, *, tm):
    def issue(t, c):
        for k in range(TOP_K):
            slot = slot_ref[0, 0, t * TOP_K + k]
            pltpu.make_async_copy(x_ref.at[pl.ds(pl.multiple_of(t * ROW_TILE, ROW_TILE), ROW_TILE)],
                                  y_hbm.at[pl.ds(pl.multiple_of(slot * ROW_TILE, ROW_TILE), ROW_TILE)],
                                  sem).start(priority=k % 2)
        return c
    lax.fori_loop(0, tm, issue, 0)

    @pl.when(pl.program_id(0) == 0)
    def _():
        zero_ref[...] = jnp.zeros_like(zero_ref)

        def fill_copy(r):
            return pltpu.make_async_copy(zero_ref, y_hbm.at[pl.ds(pl.multiple_of(r * ROW_TILE, ROW_TILE), ROW_TILE)],
                                         fill_sem)

        def per_expert(e, c):
            lax.fori_loop(0, fill_n_ref[e], lambda j, c2: (fill_copy(fill_start_ref[e] + j).start(), c2)[1], 0)
            lax.fori_loop(0, fill_n_ref[e], lambda j, c2: (fill_copy(fill_start_ref[e] + j).wait(), c2)[1], 0)
            return c
        lax.fori_loop(0, N_EXPERTS, per_expert, 0)

    x1 = x1_ref[...]
    xb = x1.astype(BF16)
    hid = (jax.nn.silu(_dot(xb, ws_gate_ref[...])) * _dot(xb, ws_up_ref[...])).astype(BF16)
    shared = _dot(hid, ws_down_ref[...])
    ple = jax.nn.sigmoid(_dot(xb, w_pg_ref[...])) * _dot(p_ref[...].astype(BF16), w_ple_ref[...])
    base_ref[...] = ALPHA * x1 + shared + ple

    rows = y_hbm.at[pl.ds(0, TOP_K * tm * ROW_TILE)]
    pltpu.make_async_copy(rows, rows, sem).wait()


def _dispatch(x1p, slot, fill_start, fill_n, x1, p, wl, n_slots, tm):
    t = x1p.shape[0] // ROW_TILE
    assert TOP_K * tm <= n_slots
    row = lambda w: pl.BlockSpec((tm, w), lambda i, *_: (i, 0))
    consts = (wl['ws_gate'], wl['ws_up'], wl['ws_down'], wl['w_ple_gate'], wl['w_ple'])
    return pl.pallas_call(
        functools.partial(_dispatch_kernel, tm=tm),
        grid_spec=pltpu.PrefetchScalarGridSpec(
            num_scalar_prefetch=2,
            grid=(t // tm,),
            in_specs=[pl.BlockSpec((1, 1, tm * TOP_K), lambda i, *_: (i, 0, 0), memory_space=pltpu.SMEM),
                      pl.BlockSpec((tm * ROW_TILE, LANES), lambda i, *_: (i, 0)), row(D_MODEL), row(PLE_DIM)]
            + [_const_spec(c.shape) for c in consts],
            out_specs=[pl.BlockSpec(memory_space=pl.ANY), row(D_MODEL)],
            scratch_shapes=[pltpu.VMEM((ROW_TILE, LANES), U32), pltpu.SemaphoreType.DMA, pltpu.SemaphoreType.DMA]),
        out_shape=[jax.ShapeDtypeStruct((n_slots * ROW_TILE, LANES), U32), jax.ShapeDtypeStruct((t, D_MODEL), F32)],
        compiler_params=_params(("arbitrary",)),
        name="dispatch",
    )(fill_start, fill_n, slot, x1p, x1, p, *consts)


def _experts_kernel(blk_e_ref, n_used_ref, y_ref, wg_ref, wu_ref, wd_ref, o_ref, wg_s, wu_s, wd_s):
    b = pl.program_id(0)

    @pl.when(b < n_used_ref[0])
    def _():
        @pl.when((b == 0) | (blk_e_ref[b] != blk_e_ref[jnp.maximum(b - 1, 0)]))
        def _():
            wg_s[...] = wg_ref[0, 0].astype(BF16)
            wu_s[...] = wu_ref[0, 0].astype(BF16)
            wd_s[...] = wd_ref[0, 0].astype(BF16)

        bm = y_ref.shape[0] // ROW_TILE
        halves = [_unpack_pair(u) for u in _load_row_tiles(y_ref, bm)]
        x = jnp.concatenate([lo.astype(BF16) for lo, _ in halves] + [hi.astype(BF16) for _, hi in halves], axis=1)
        hid = (jax.nn.silu(_dot(x, wg_s[...])) * _dot(x, wu_s[...])).astype(BF16)
        out = _dot(hid, wd_s[...])
        _store_row_tiles(o_ref, _pack_pair(out[:, :HALF], out[:, HALF:]))


def _experts(y_in, blk_e, n_used, w_gate, w_up, w_down, layer, bm):
    n_slots = y_in.shape[0] // ROW_TILE
    blk = lambda b, be, nu: (jnp.minimum(b, nu[0] - 1), 0)
    wsel = lambda b, be, nu: (layer, be[b], 0, 0)
    return pl.pallas_call(
        _experts_kernel,
        grid_spec=pltpu.PrefetchScalarGridSpec(
            num_scalar_prefetch=2,
            grid=(n_slots // bm,),
            in_specs=[pl.BlockSpec((bm * ROW_TILE, LANES), blk),
                      pl.BlockSpec((1, 1, D_MODEL, D_EXPERT), wsel),
                      pl.BlockSpec((1, 1, D_MODEL, D_EXPERT), wsel),
                      pl.BlockSpec((1, 1, D_EXPERT, D_MODEL), wsel)],
            out_specs=pl.BlockSpec((bm * ROW_TILE, LANES), blk),
            scratch_shapes=[pltpu.VMEM((D_MODEL, D_EXPERT), BF16), pltpu.VMEM((D_MODEL, D_EXPERT), BF16),
                            pltpu.VMEM((D_EXPERT, D_MODEL), BF16)]),
        out_shape=jax.ShapeDtypeStruct((n_slots * ROW_TILE, LANES), U32),
        compiler_params=_params(("arbitrary",)),
        name="experts",
    )(blk_e, n_used, y_in, w_gate, w_up, w_down)


def _final_kernel(slot_ref, slot_next_ref, y_hbm, w_ref, base_ref, ln_g_ref, ln_b_ref, o_ref, rows_ref, sem, *, tm):
    i = pl.program_id(0)
    cur = i % 2

    def gather(ids_ref, buf):
        def issue(pair, c):
            for t in (2 * pair, 2 * pair + 1):
                dst_row = pl.multiple_of(t * ROW_TILE, ROW_TILE)
                for k in range(TOP_K):
                    src_row = pl.multiple_of(ids_ref[0, 0, t * TOP_K + k] * ROW_TILE, ROW_TILE)
                    pltpu.make_async_copy(y_hbm.at[pl.ds(src_row, ROW_TILE)],
                                          rows_ref.at[buf, k, pl.ds(dst_row, ROW_TILE)], sem.at[buf]).start()
            return c
        lax.fori_loop(0, tm // 2, issue, 0)

    @pl.when(i == 0)
    def _():
        gather(slot_ref, 0)

    has_next = i + 1 < pl.num_programs(0)
    for buf in range(2):
        @pl.when(has_next & (cur != buf))
        def _():
            gather(slot_next_ref, buf)

    pltpu.make_async_copy(rows_ref.at[cur], rows_ref.at[cur], sem.at[cur]).wait()

    w = w_ref[...]
    wk = [jnp.broadcast_to(w[:, k:k + 1], (tm, LANES)) for k in range(TOP_K)]
    r_lo, r_hi = [], []
    for j in range(ROW_TILE):
        a_lo = jnp.zeros((tm, LANES), F32)
        a_hi = jnp.zeros((tm, LANES), F32)
        for k in range(TOP_K):
            lo, hi = _unpack_pair(rows_ref[cur, k, pl.ds(j, tm, stride=ROW_TILE), :])
            a_lo += wk[k] * lo
            a_hi += wk[k] * hi
        r_lo.append(a_lo)
        r_hi.append(a_hi)
    y = base_ref[...] + jnp.concatenate(r_lo + r_hi, axis=1)
    o_ref[...] = _layer_norm(y, ln_g_ref[...], ln_b_ref[...])


def _final(slot, y_out, w_tok, base, wl, tm, tile0, n):
    row = lambda w: pl.BlockSpec((tm, w), lambda i: (tile0 + i, 0))
    ids = lambda index: pl.BlockSpec((1, 1, tm * TOP_K), index, memory_space=pltpu.SMEM)
    consts = (wl['ln2_g'], wl['ln2_b'])
    return pl.pallas_call(
        functools.partial(_final_kernel, tm=tm),
        grid=(n,),
        in_specs=[ids(lambda i: (tile0 + i, 0, 0)), ids(lambda i: (tile0 + jnp.minimum(i + 1, n - 1), 0, 0)),
                  pl.BlockSpec(memory_space=pl.ANY),
                  row(TOP_K), row(D_MODEL)] + [_const_spec(c.shape) for c in consts],
        out_specs=pl.BlockSpec((tm, D_MODEL), lambda i: (i, 0)),
        out_shape=jax.ShapeDtypeStruct((n * tm, D_MODEL), F32),
        scratch_shapes=[pltpu.VMEM((2, TOP_K, tm * ROW_TILE, LANES), U32), pltpu.SemaphoreType.DMA((2,))],
        compiler_params=_params(("arbitrary",)),
        name="final",
    )(slot, slot, y_out, w_tok, base, *consts)


def _rope_tables(n):
    t = jnp.arange(n, dtype=F32)
    inv = ROPE_THETA ** (-jnp.arange(0, MLA_ROPE, 2, dtype=F32) / MLA_ROPE)
    a_seq = t[:, None] * inv[None, :]
    a_row = jnp.floor(t / GRID_W)[:, None] * inv[None, :]
    a_col = (t - jnp.floor(t / GRID_W) * GRID_W)[:, None] * inv[None, :]
    z = jnp.zeros((n, LANES // 2), F32)
    cos_m = jnp.concatenate([jnp.cos(a_seq), jnp.cos(a_seq), z], axis=1)
    sin_m = jnp.concatenate([-jnp.sin(a_seq), jnp.sin(a_seq), z], axis=1)
    cos_a = jnp.concatenate([jnp.cos(a_row), jnp.cos(a_row), jnp.cos(a_col), jnp.cos(a_col)], axis=1)
    sin_a = jnp.concatenate([-jnp.sin(a_row), jnp.sin(a_row), -jnp.sin(a_col), jnp.sin(a_col)], axis=1)
    return cos_m, sin_m, cos_a, sin_a


def _layer_weights(i, w_in, g_q_lat, g_kv_lat, w_uq, w_uk, w_uv, g_q_gqa, g_k_gqa, g_o_mla, g_o_gqa,
                   w_o, ln1_g, ln1_b, w_router, router_bias, w_gate, w_up, w_down, ws_gate, ws_up,
                   ws_down, w_ple_gate, w_ple, ln2_g, ln2_b):
    wi = w_in[i]
    kr0 = Q_LORA + KV_LORA
    w_in_p = jnp.concatenate(
        [wi[:, :kr0 + MLA_ROPE], jnp.zeros((D_MODEL, LANES - MLA_ROPE), F32), wi[:, kr0 + MLA_ROPE:]], axis=1)
    uq = w_uq[i].reshape(Q_LORA, MLA_HEADS, MLA_NOPE + MLA_ROPE)
    uq = jnp.pad(uq, ((0, 0), (0, 0), (0, MLA_QK_PAD - MLA_NOPE - MLA_ROPE)))
    vec = lambda a: a[i].reshape(1, -1).astype(F32)
    return dict(
        w_in_p=w_in_p.astype(BF16), w_uq_p=uq.reshape(Q_LORA, MLA_HEADS * MLA_QK_PAD).astype(BF16),
        w_uk=w_uk[i].astype(BF16), w_uv=w_uv[i].astype(BF16),
        g_q_lat=vec(g_q_lat), g_kv_lat=vec(g_kv_lat), g_q_gqa=vec(g_q_gqa), g_k_gqa=vec(g_k_gqa),
        g_o_mla=vec(g_o_mla), g_o_gqa=vec(g_o_gqa), w_o=w_o[i].astype(BF16),
        ln1_g=vec(ln1_g), ln1_b=vec(ln1_b),
        w_rt=w_router[i].T.astype(F32), bias=router_bias[i].reshape(-1, 1).astype(F32),
        ws_gate=ws_gate[i].astype(BF16), ws_up=ws_up[i].astype(BF16), ws_down=ws_down[i].astype(BF16),
        w_ple_gate=w_ple_gate[i].astype(BF16), w_ple=w_ple[i].astype(BF16),
        ln2_g=vec(ln2_g), ln2_b=vec(ln2_b))


def _slot_plan(counts, bm, n_blocks):
    counts = counts.astype(I32)
    padded = (counts + bm - 1) // bm * bm
    pad_end = jnp.cumsum(padded)
    pad_start = pad_end - padded
    first_row = jnp.arange(n_blocks, dtype=I32) * bm
    blk_e = jnp.minimum(jnp.sum(pad_end[None, :] <= first_row[:, None], axis=1), N_EXPERTS - 1).astype(I32)
    n_used = jnp.maximum(pad_end[-1:] // bm, 1).astype(I32)
    return pad_start, pad_start + counts, padded - counts, blk_e, n_used


def kernel(x_prompt, x_sample, p_prompt, p_sample, w_in, g_q_lat, g_kv_lat, w_uq, w_uk, w_uv,
           g_q_gqa, g_k_gqa, g_o_mla, g_o_gqa, w_o, ln1_g, ln1_b, w_router, router_bias,
           w_gate, w_up, w_down, ws_gate, ws_up, ws_down, w_ple_gate, w_ple, ln2_g, ln2_b):
    weights = (w_in, g_q_lat, g_kv_lat, w_uq, w_uk, w_uv, g_q_gqa, g_k_gqa, g_o_mla, g_o_gqa, w_o,
               ln1_g, ln1_b, w_router, router_bias, w_gate, w_up, w_down, ws_gate, ws_up, ws_down,
               w_ple_gate, w_ple, ln2_g, ln2_b)
    b1, s1, _ = x_prompt.shape
    b2, s2, _ = x_sample.shape
    t1, t2 = b1 * s1, b2 * s2
    t = t1 + t2
    x = jnp.concatenate([x_prompt.reshape(t1, D_MODEL), x_sample.reshape(t2, D_MODEL)], axis=0)
    p = jnp.concatenate([p_prompt.reshape(DEPTH, t1, PLE_DIM), p_sample.reshape(DEPTH, t2, PLE_DIM)], axis=1)

    tm = _tile(min(s1, s2), 512)
    tm_final = _tile(min(s1, s2), 256)
    bm = _tile(t * TOP_K // N_EXPERTS, 512)
    n_blocks = t * TOP_K // bm + N_EXPERTS
    tables = _rope_tables(max(s1, s2))

    def pos_block(i):
        return jnp.where(i < t1 // tm, i % (s1 // tm), (i - t1 // tm) % (s2 // tm))

    for layer in range(DEPTH):
        wl = _layer_weights(layer, *weights)
        q_mla, k_mla, v_mla, q_gqa, k_gqa, v_gqa = _proj(x, wl, tables, pos_block, tm)
        groups = ((0, b1, s1), (t1, b2, s2))
        o_mla = o_gqa = None
        for r0, nb, s in groups:
            o_mla = _attention(q_mla, k_mla, v_mla, o_mla, row0=r0, n_seq=nb, seq=s, heads=MLA_HEADS, kv_group=1,
                               dq=MLA_QK_PAD, dv=MLA_V, tq=_tile(s, 512), tk=_tile(s, 2048), name="attn_mla")
            o_gqa = _attention(q_gqa, k_gqa, v_gqa, o_gqa, row0=r0, n_seq=nb, seq=s, heads=GQA_HEADS,
                               kv_group=GQA_GROUP, dq=GQA_HD, dv=GQA_HD, tq=_tile(s, 512), tk=_tile(s, 2048),
                               name="attn_gqa")
        x1, x1p = _outproj(x, o_mla, o_gqa, wl, tm)

        mask, gate, counts = _route(x1, wl['w_rt'], wl['bias'], tm)
        pad_start, fill_start, fill_n, blk_e, n_used = _slot_plan(counts[:, 0], bm, n_blocks)
        slot, w_k = _slots(mask, gate, pad_start.astype(F32).reshape(-1, 1), tm)
        slot_tok = slot.T
        y_in, base = _dispatch(x1p, slot_tok.reshape(t // tm, 1, tm * TOP_K), fill_start, fill_n, x1, p[layer], wl,
                               n_blocks * bm, tm)
        y_out = _experts(y_in, blk_e, n_used, w_gate, w_up, w_down, layer, bm)
        final = functools.partial(_final, slot_tok.reshape(t // tm_final, 1, tm_final * TOP_K), y_out, w_k.T, base,
                                  wl, tm_final)
        if layer < DEPTH - 1:
            x = final(0, t // tm_final)
        else:
            y_prompt = final(0, t1 // tm_final)
            y_sample = final(t1 // tm_final, t2 // tm_final)

    return y_prompt.reshape(b1, s1, D_MODEL), y_sample.reshape(b2, s2, D_MODEL)
```
